```python
import jax, jax.numpy as jnp
from jax import lax
import numpy as np

D_MODEL = 1024
BATCH = 4
SEQ = 4096
DEPTH = 1

D_MIX = D_MODEL
CONV_WIDTH = D_MIX // 2
CONV_K = 3
ATTN_HEADS = 8
HEAD_DIM = 64
ATTN_WIDTH = ATTN_HEADS * HEAD_DIM
IDX_HEADS = 8
IDX_DIM = 64
TOPK_MAX = 256
Q_BLOCK = 128
GROUP_DIM = 64
N_GROUPS = D_MIX // GROUP_DIM
NORM_EPS = 1e-6
NEG_INF = -1e30
IN_SPLIT_SIZES = (CONV_WIDTH, CONV_WIDTH, CONV_WIDTH, CONV_WIDTH,
                  ATTN_WIDTH, ATTN_WIDTH, ATTN_WIDTH, ATTN_WIDTH,
                  IDX_HEADS * IDX_DIM, IDX_DIM, IDX_HEADS)
IN_COLS = sum(IN_SPLIT_SIZES)

kernel_name = "hybrid_shortconv_dsa_alibi_adaln"


def rmsnorm(x, eps=NORM_EPS):
    xf = x.astype(jnp.float32)
    return (xf * lax.rsqrt(jnp.mean(xf * xf, axis=-1, keepdims=True) + eps)).astype(x.dtype)


def layernorm(x, gain, bias, eps=NORM_EPS):
    xf = x.astype(jnp.float32)
    mu = jnp.mean(xf, axis=-1, keepdims=True)
    var = jnp.mean(jnp.square(xf - mu), axis=-1, keepdims=True)
    return ((xf - mu) * lax.rsqrt(var + eps)).astype(x.dtype) * gain + bias


def group_rmsnorm(y, gain):
    B, S, _ = y.shape
    yg = rmsnorm(y.reshape(B, S, N_GROUPS, GROUP_DIM)).reshape(B, S, D_MIX)
    return yg * gain


def alibi_slopes(n_heads):
    return np.asarray([2.0 ** (-8.0 * (i + 1) / n_heads) for i in range(n_heads)], dtype=np.float32)


def short_conv(xin, b_gate, c_gate, conv_w, conv_b):
    u = c_gate * xin
    S = u.shape[1]
    up = jnp.pad(u, ((0, 0), (CONV_K - 1, 0), (0, 0)))
    y = conv_b
    for j in range(CONV_K):
        y = y + conv_w[j] * up[:, j:j + S]
    return b_gate * y


def dsa_attention(q, k, v, q_idx, k_idx, w_idx):
    B, S = q.shape[0], q.shape[1]
    topk = min(TOPK_MAX, S // 4)
    nb = S // Q_BLOCK
    slopes = jnp.asarray(alibi_slopes(ATTN_HEADS))
    key_pos = jnp.arange(S, dtype=jnp.int32)
    idx_scale = (IDX_DIM ** -0.5) * (IDX_HEADS ** -0.5)
    attn_scale = HEAD_DIM ** -0.5
    gather = jax.vmap(lambda kv, ix: kv[ix])

    def to_blocks(a):
        return a.reshape((B, nb, Q_BLOCK) + a.shape[2:]).swapaxes(0, 1)

    def one_block(args):
        qb, qib, wb, start = args
        q_pos = start + jnp.arange(Q_BLOCK, dtype=jnp.int32)
        causal = key_pos[None, :] <= q_pos[:, None]
        rel = jax.nn.relu(jnp.einsum('bqhd,bsd->bqhs', qib, k_idx).astype(jnp.float32))
        score = jnp.einsum('bqh,bqhs->bqs', wb.astype(jnp.float32) * idx_scale, rel)
        score = jnp.where(causal[None], score, NEG_INF)
        _, sel = lax.top_k(score, topk)
        valid = sel <= q_pos[None, :, None]
        k_sel = gather(k, sel)
        v_sel = gather(v, sel)
        logits = jnp.einsum('bqhd,bqkhd->bqhk', qb, k_sel).astype(jnp.float32) * attn_scale
        dist = (q_pos[None, :, None] - sel).astype(jnp.float32)
        logits = logits - slopes[None, None, :, None] * dist[:, :, None, :]
        logits = jnp.where(valid[:, :, None, :], logits, NEG_INF)
        p = jax.nn.softmax(logits, axis=-1).astype(v.dtype)
        return jnp.einsum('bqhk,bqkhd->bqhd', p, v_sel)

    starts = jnp.arange(nb, dtype=jnp.int32) * Q_BLOCK
    out = lax.map(one_block, (to_blocks(q), to_blocks(q_idx), to_blocks(w_idx), starts))
    return out.swapaxes(0, 1).reshape(B, S, ATTN_HEADS, HEAD_DIM)


def setup_inputs(seed: int = 0) -> dict:
    key = jax.random.key(seed)
    ks = jax.random.split(key, 12)
    f32 = jnp.float32
    x = jax.random.normal(ks[0], (BATCH, SEQ, D_MODEL), f32)
    c = jax.random.normal(ks[1], (BATCH, D_MODEL), f32)
    w_ada = jax.random.normal(ks[2], (DEPTH, D_MODEL, 3 * D_MODEL), f32) * (0.5 * D_MODEL ** -0.5)
    b_ada = jax.random.normal(ks[3], (DEPTH, 3 * D_MODEL), f32) * 0.02
    w_in = jax.random.normal(ks[4], (DEPTH, D_MODEL, IN_COLS), f32) * (D_MODEL ** -0.5)
    conv_w = jax.random.normal(ks[5], (DEPTH, CONV_K, CONV_WIDTH), f32) * (CONV_K ** -0.5)
    conv_b = jax.random.normal(ks[6], (DEPTH, CONV_WIDTH), f32) * 0.02
    idx_k_gain = 1.0 + 0.02 * jax.random.normal(ks[7], (DEPTH, IDX_DIM), f32)
    idx_k_bias = 0.02 * jax.random.normal(ks[8], (DEPTH, IDX_DIM), f32)
    mix_norm_gain = 1.0 + 0.02 * jax.random.normal(ks[9], (DEPTH, D_MIX), f32)
    w_out = jax.random.normal(ks[10], (DEPTH, D_MIX, D_MODEL), f32) * (D_MIX ** -0.5)
    final_gain = 1.0 + 0.02 * jax.random.normal(ks[11], (D_MODEL,), f32)
    return {"x": x, "c": c, "w_ada": w_ada, "b_ada": b_ada, "w_in": w_in,
            "conv_w": conv_w, "conv_b": conv_b, "idx_k_gain": idx_k_gain,
            "idx_k_bias": idx_k_bias, "mix_norm_gain": mix_norm_gain,
            "w_out": w_out, "final_gain": final_gain}


def reference(x, c, w_ada, b_ada, w_in, conv_w, conv_b, idx_k_gain, idx_k_bias,
              mix_norm_gain, w_out, final_gain):
    B, S, _ = x.shape
    offsets = [int(o) for o in np.cumsum(IN_SPLIT_SIZES)[:-1]]
    c_act = jax.nn.silu(c)
    for l in range(DEPTH):
        mod = c_act @ w_ada[l] + b_ada[l]
        shift, scale, gate = jnp.split(mod, 3, axis=-1)
        h = rmsnorm(x) * (1.0 + scale[:, None, :]) + shift[:, None, :]
        proj = h @ w_in[l]
        (cb, cc, cx, cz, q, k, v, az, qi, ki, wi) = jnp.split(proj, offsets, axis=-1)
        conv_out = short_conv(cx, cb, cc, conv_w[l], conv_b[l])
        q = q.reshape(B, S, ATTN_HEADS, HEAD_DIM)
        k = k.reshape(B, S, ATTN_HEADS, HEAD_DIM)
        v = v.reshape(B, S, ATTN_HEADS, HEAD_DIM)
        qi = qi.reshape(B, S, IDX_HEADS, IDX_DIM)
        ki = layernorm(ki, idx_k_gain[l], idx_k_bias[l])
        attn_out = dsa_attention(q, k, v, qi, ki, wi).reshape(B, S, ATTN_WIDTH)
        y = jnp.concatenate([conv_out, attn_out], axis=-1)
        z = jnp.concatenate([cz, az], axis=-1)
        y = group_rmsnorm(y, mix_norm_gain[l]) * jax.nn.silu(z)
        x = x + gate[:, None, :] * (y @ w_out[l])
    return rmsnorm(x) * final_gain
```

```python
import functools

import numpy as np
import jax
import jax.numpy as jnp
from jax import lax
from jax.experimental import pallas as pl
from jax.experimental.pallas import tpu as pltpu

F32 = jnp.float32
BF16 = jnp.bfloat16
I32 = jnp.int32

D_MODEL = 1024
CONV_WIDTH = 512
CONV_K = 3
ATTN_HEADS = 8
HEAD_DIM = 64
ATTN_WIDTH = ATTN_HEADS * HEAD_DIM
IDX_HEADS = 8
IDX_DIM = 64
TOPK_MAX = 256
GROUP_DIM = 64
NORM_EPS = 1e-6
NEG_INF = -1e30

LANES = 128
SUBLANES = 8
VMEM_LIMIT_BYTES = 56 * 1024 * 1024

TM = 512
TQ = 128
CK = 512
LOG2_CK = 9
N_PAIRS = ATTN_HEADS // 2

INT_MIN = -(2 ** 31)
_NEG_BITS = int(np.float32(NEG_INF).view(np.int32))
KEY_NEG = _NEG_BITS ^ 0x7FFFFFFF

NT_DIMS = (((1,), (1,)), ((), ()))


def _alibi_slopes(n_heads):
    return [2.0 ** (-8.0 * (i + 1) / n_heads) for i in range(n_heads)]


def _mod_kernel(c_ref, w_ref, b_ref, o_ref):
    c_act = jax.nn.silu(c_ref[...])
    o_ref[...] = jnp.dot(c_act.astype(BF16), w_ref[...].astype(BF16),
                         preferred_element_type=F32) + b_ref[...]


def _modulation(c_pad, w_ada, b_ada):
    rows, d = c_pad.shape
    n_out = w_ada.shape[1]
    return pl.pallas_call(
        _mod_kernel,
        grid=(n_out // d,),
        in_specs=[
            pl.BlockSpec((rows, d), lambda j: (0, 0)),
            pl.BlockSpec((d, d), lambda j: (0, j)),
            pl.BlockSpec((1, d), lambda j: (0, j)),
        ],
        out_specs=pl.BlockSpec((rows, d), lambda j: (0, j)),
        out_shape=jax.ShapeDtypeStruct((rows, n_out), F32),
        compiler_params=pltpu.CompilerParams(
            dimension_semantics=("arbitrary",), vmem_limit_bytes=VMEM_LIMIT_BYTES),
        name="adaln_mod",
    )(c_pad, w_ada, b_ada)


_C_CONV = 0
_C_Q = 4 * CONV_WIDTH
_C_K = _C_Q + ATTN_WIDTH
_C_AZ = _C_K + ATTN_WIDTH
_C_QI = _C_AZ + ATTN_WIDTH
_C_KI = _C_QI + IDX_HEADS * IDX_DIM
_W1_COLS = _C_KI + 2 * IDX_DIM
_W2_ROWS = ATTN_WIDTH + 16


def _proj_kernel(x_ref, mod_ref, w1_ref, w2t_ref, cw_ref, cb_ref, lng_ref, lnb_ref,
                 gconv_ref, gmat_ref,
                 yc_ref, q_ref, k_ref, gz_ref, qi_ref, ki2_ref, vt_ref, wit_ref,
                 u_scr):
    i = pl.program_id(1)
    x = x_ref[...]
    ms = jnp.mean(x * x, axis=-1, keepdims=True)
    xn = x * lax.rsqrt(ms + NORM_EPS)
    shift = mod_ref[0:1, :]
    scale = mod_ref[1:2, :]
    h = (xn * (1.0 + scale) + shift).astype(BF16)

    def proj(c0, width):
        return jnp.dot(h, w1_ref[:, c0:c0 + width], preferred_element_type=F32)

    pc = proj(_C_CONV, 4 * CONV_WIDTH)
    b_gate = pc[:, 0:CONV_WIDTH]
    c_gate = pc[:, CONV_WIDTH:2 * CONV_WIDTH]
    x_in = pc[:, 2 * CONV_WIDTH:3 * CONV_WIDTH]
    z_conv = pc[:, 3 * CONV_WIDTH:4 * CONV_WIDTH]
    u = c_gate * x_in

    @pl.when(i == 0)
    def _():
        u_scr[0:SUBLANES, :] = jnp.zeros((SUBLANES, CONV_WIDTH), F32)

    u_scr[SUBLANES:SUBLANES + TM, :] = u
    u_m1 = u_scr[SUBLANES - 1:SUBLANES - 1 + TM, :]
    u_m2 = u_scr[SUBLANES - 2:SUBLANES - 2 + TM, :]
    y = cb_ref[...] + cw_ref[0:1, :] * u_m2
    y = y + cw_ref[1:2, :] * u_m1
    y = y + cw_ref[2:3, :] * u
    conv_out = b_gate * y
    u_scr[0:SUBLANES, :] = u_scr[TM:TM + SUBLANES, :]

    sq = conv_out * conv_out
    sq_hi = sq.astype(BF16)
    sq_lo = (sq - sq_hi.astype(F32)).astype(BF16)
    gms = (jnp.dot(sq_hi, gmat_ref[...], preferred_element_type=F32)
           + jnp.dot(sq_lo, gmat_ref[...], preferred_element_type=F32))
    yc = conv_out * lax.rsqrt(gms + NORM_EPS) * gconv_ref[...] * jax.nn.silu(z_conv)
    yc_ref[...] = yc.astype(BF16)

    q_ref[...] = (proj(_C_Q, ATTN_WIDTH) * (HEAD_DIM ** -0.5)).astype(BF16)
    k_ref[...] = proj(_C_K, ATTN_WIDTH).astype(BF16)
    gz_ref[...] = jax.nn.silu(proj(_C_AZ, ATTN_WIDTH))
    qi_ref[...] = proj(_C_QI, IDX_HEADS * IDX_DIM).astype(BF16)

    pki = proj(_C_KI, 2 * IDX_DIM)
    mu = jnp.mean(pki, axis=-1, keepdims=True)
    var = jnp.mean(jnp.square(pki - mu), axis=-1, keepdims=True)
    ki = (pki - mu) * lax.rsqrt(var + NORM_EPS) * lng_ref[...] + lnb_ref[...]
    ki2_ref[...] = ki.astype(BF16)

    pt = lax.dot_general(w2t_ref[...], h, NT_DIMS, preferred_element_type=F32)
    vt_ref[...] = pt[0:ATTN_WIDTH, :].astype(BF16)
    idx_scale = (IDX_DIM ** -0.5) * (IDX_HEADS ** -0.5)
    wit_ref[...] = pt[ATTN_WIDTH:ATTN_WIDTH + IDX_HEADS, :] * idx_scale


def _projection(x, mod3, w1, w2t, conv_w, conv_b, lng2, lnb2, gconv, gmat):
    B, S, D = x.shape
    n_t = S // TM
    const2 = lambda b, i: (0, 0)
    out_shapes = (
        jax.ShapeDtypeStruct((B, S, CONV_WIDTH), BF16),
        jax.ShapeDtypeStruct((B, S, ATTN_WIDTH), BF16),
        jax.ShapeDtypeStruct((B, S, ATTN_WIDTH), BF16),
        jax.ShapeDtypeStruct((B, S, ATTN_WIDTH), F32),
        jax.ShapeDtypeStruct((B, S, IDX_HEADS * IDX_DIM), BF16),
        jax.ShapeDtypeStruct((B, S, 2 * IDX_DIM), BF16),
        jax.ShapeDtypeStruct((B, n_t, ATTN_WIDTH, TM), BF16),
        jax.ShapeDtypeStruct((B, IDX_HEADS, S), F32),
    )
    tok = lambda w: pl.BlockSpec((None, TM, w), lambda b, i: (b, i, 0))
    return pl.pallas_call(
        _proj_kernel,
        grid=(B, n_t),
        in_specs=[
            pl.BlockSpec((None, TM, D), lambda b, i: (b, i, 0)),
            pl.BlockSpec((None, 3, D), lambda b, i: (b, 0, 0)),
            pl.BlockSpec((D, _W1_COLS), const2),
            pl.BlockSpec((_W2_ROWS, D), const2),
            pl.BlockSpec((CONV_K, CONV_WIDTH), const2),
            pl.BlockSpec((1, CONV_WIDTH), const2),
            pl.BlockSpec((1, 2 * IDX_DIM), const2),
            pl.BlockSpec((1, 2 * IDX_DIM), const2),
            pl.BlockSpec((1, CONV_WIDTH), const2),
            pl.BlockSpec((CONV_WIDTH, CONV_WIDTH), const2),
        ],
        out_specs=(
            tok(CONV_WIDTH), tok(ATTN_WIDTH), tok(ATTN_WIDTH), tok(ATTN_WIDTH),
            tok(IDX_HEADS * IDX_DIM), tok(2 * IDX_DIM),
            pl.BlockSpec((None, None, ATTN_WIDTH, TM), lambda b, i: (b, i, 0, 0)),
            pl.BlockSpec((None, IDX_HEADS, TM), lambda b, i: (b, 0, i)),
        ),
        out_shape=out_shapes,
        scratch_shapes=[pltpu.VMEM((TM + 2 * SUBLANES, CONV_WIDTH), F32)],
        compiler_params=pltpu.CompilerParams(
            dimension_semantics=("arbitrary", "arbitrary"),
            vmem_limit_bytes=VMEM_LIMIT_BYTES),
        name="proj_conv",
    )(x, mod3, w1, w2t, conv_w, conv_b, lng2, lnb2, gconv, gmat)


def _attn_kernel(topk, seq_len,
                 x_ref, yc_ref, gz_ref, q_ref, qi_ref, wit_ref, k_ref, vt_ref, ki2_ref,
                 gate_ref, wout_ref, gattn_ref, gfin_ref,
                 o_ref,
                 keys_scr, mb_scr, acc_scr, m_scr, l_scr, qiw_scr, qw_scr):
    qb = pl.program_id(1)
    t0 = qb * TQ
    n_chunks = lax.shift_right_logical(t0 + (TQ - 1), LOG2_CK) + 1
    slopes = _alibi_slopes(ATTN_HEADS)

    lane = lax.broadcasted_iota(I32, (TQ, LANES), 1)
    lo_half = lane < HEAD_DIM
    for j in range(N_PAIRS):
        for src, dst in ((qi_ref, qiw_scr), (q_ref, qw_scr)):
            pair = src[:, j * LANES:(j + 1) * LANES]
            zero = jnp.zeros_like(pair)
            dst[j, 0:TQ, :] = jnp.where(lo_half, pair, zero)
            dst[j, TQ:2 * TQ, :] = jnp.where(lo_half, zero, pair)

    row = lax.broadcasted_iota(I32, (CK, TQ), 0)
    t_idx = t0 + lax.broadcasted_iota(I32, (CK, TQ), 1)

    def score_body(c, carry):
        base = pl.multiple_of(c * CK, CK)
        kic = ki2_ref[pl.ds(base, CK), :]
        score = None
        for j in range(N_PAIRS):
            d = lax.dot_general(kic, qiw_scr[j], NT_DIMS, preferred_element_type=F32)
            r = jnp.maximum(d, 0.0)
            term = (r[:, 0:TQ] * wit_ref[2 * j:2 * j + 1, :]
                    + r[:, TQ:2 * TQ] * wit_ref[2 * j + 1:2 * j + 2, :])
            score = term if score is None else score + term
        bits = pltpu.bitcast(score, I32)
        key = jnp.where(bits >= 0, bits, bits ^ 0x7FFFFFFF)
        key = jnp.where(base + row <= t_idx, key, INT_MIN)
        keys_scr[pl.ds(base, CK), :] = key
        return carry

    lax.fori_loop(0, n_chunks, score_body, 0)

    n_acc = 4

    def count_rows(pred):
        def body(c, accs):
            base = pl.multiple_of(c * CK, CK)
            accs = list(accs)
            for r in range(CK // SUBLANES):
                kv = keys_scr[pl.ds(base + r * SUBLANES, SUBLANES), :]
                accs[r % n_acc] = accs[r % n_acc] + pred(kv, base + r * SUBLANES)
            return tuple(accs)
        zero = jnp.zeros((SUBLANES, TQ), I32)
        accs = lax.fori_loop(0, n_chunks, body, (zero,) * n_acc)
        tot = (accs[0] + accs[1]) + (accs[2] + accs[3])
        return jnp.sum(tot, axis=0, keepdims=True)

    n_noncausal = (seq_len - 1) - (t0 + lax.broadcasted_iota(I32, (1, TQ), 1))

    def bit_body(ib, carry):
        u_thr, cnt_thr = carry
        cand_u = u_thr | lax.shift_left(jnp.int32(1), 31 - ib)
        cand = cand_u ^ INT_MIN
        cnt = count_rows(lambda kv, _: jnp.where(kv >= cand, 1, 0))
        total = cnt + jnp.where(cand <= KEY_NEG, n_noncausal, 0)
        ok = total >= topk
        return jnp.where(ok, cand_u, u_thr), jnp.where(ok, cnt, cnt_thr)

    zero_row = jnp.zeros((1, TQ), I32)
    u_thr, cnt_thr = lax.fori_loop(0, 32, bit_body, (zero_row, zero_row))
    thr = u_thr ^ INT_MIN

    def mask_fast():
        def body(c, carry):
            base = pl.multiple_of(c * CK, CK)
            kv = keys_scr[pl.ds(base, CK), :]
            mb_scr[pl.ds(base, CK), :] = jnp.where(kv >= thr, 0.0, NEG_INF)
            return carry
        lax.fori_loop(0, n_chunks, body, 0)

    def mask_ties():
        sub = lax.broadcasted_iota(I32, (SUBLANES, TQ), 0)
        cnt_gt = count_rows(lambda kv, _: jnp.where(kv > thr, 1, 0))
        need = topk - cnt_gt
        n_bits = (seq_len - 1).bit_length()

        def idx_body(ib, j_thr):
            cand = j_thr | lax.shift_left(jnp.int32(1), (n_bits - 1) - ib)
            cnt = count_rows(lambda kv, r0: jnp.where(
                kv == thr, jnp.where(r0 + sub < cand, 1, 0), 0))
            return jnp.where(cnt < need, cand, j_thr)

        j_thr = lax.fori_loop(0, n_bits, idx_body, zero_row)

        def body(c, carry):
            base = pl.multiple_of(c * CK, CK)
            kv = keys_scr[pl.ds(base, CK), :]
            tie_ok = jnp.where(base + row <= j_thr, 0.0, NEG_INF)
            mb_scr[pl.ds(base, CK), :] = jnp.where(
                kv > thr, 0.0, jnp.where(kv == thr, tie_ok, NEG_INF))
            return carry
        lax.fori_loop(0, n_chunks, body, 0)

    lax.cond(jnp.max(cnt_thr) > topk, mask_ties, mask_fast)

    m_scr[...] = jnp.full((ATTN_HEADS, TQ), NEG_INF, F32)
    l_scr[...] = jnp.zeros((ATTN_HEADS, TQ), F32)
    acc_scr[...] = jnp.zeros((ATTN_WIDTH, TQ), F32)

    def attn_body(c, carry):
        base = pl.multiple_of(c * CK, CK)
        dist = (base + row - t_idx).astype(F32)
        mb = mb_scr[pl.ds(base, CK), :]
        for j in range(N_PAIRS):
            kc = k_ref[pl.ds(base, CK), j * LANES:(j + 1) * LANES]
            lg = lax.dot_general(kc, qw_scr[j], NT_DIMS, preferred_element_type=F32)
            for e in range(2):
                hd = 2 * j + e
                logit = lg[:, e * TQ:(e + 1) * TQ] + (slopes[hd] * dist + mb)
                m_old = m_scr[hd:hd + 1, :]
                m_new = jnp.maximum(m_old, jnp.max(logit, axis=0, keepdims=True))
                alpha = jnp.exp(m_old - m_new)
                p = jnp.exp(logit - m_new)
                l_scr[hd:hd + 1, :] = alpha * l_scr[hd:hd + 1, :] + jnp.sum(
                    p, axis=0, keepdims=True)
                m_scr[hd:hd + 1, :] = m_new
                rows = slice(hd * HEAD_DIM, (hd + 1) * HEAD_DIM)
                pv = jnp.dot(vt_ref[c, rows, :], p.astype(BF16),
                             preferred_element_type=F32)
                acc_scr[rows, :] = alpha * acc_scr[rows, :] + pv
        return carry

    lax.fori_loop(0, n_chunks, attn_body, 0)

    normed = []
    for hd in range(ATTN_HEADS):
        rows = slice(hd * HEAD_DIM, (hd + 1) * HEAD_DIM)
        o = acc_scr[rows, :] / l_scr[hd:hd + 1, :]
        ms = jnp.mean(o * o, axis=0, keepdims=True)
        normed.append(o * lax.rsqrt(ms + NORM_EPS))
    attn_t = jnp.concatenate(normed, axis=0)
    ya = attn_t.T * gattn_ref[...] * gz_ref[...]
    upd = (jnp.dot(yc_ref[...], wout_ref[0:CONV_WIDTH, :], preferred_element_type=F32)
           + jnp.dot(ya.astype(BF16), wout_ref[CONV_WIDTH:, :],
                     preferred_element_type=F32))
    xr = x_ref[...] + gate_ref[...] * upd
    ms = jnp.mean(xr * xr, axis=-1, keepdims=True)
    o_ref[...] = xr * lax.rsqrt(ms + NORM_EPS) * gfin_ref[...]


def _attention(x, yc, gz, q, qi, wit, k, vt, ki2, gate, wout, gattn, gfin):
    B, S, D = x.shape
    topk = min(TOPK_MAX, S // 4)
    n_q = S // TQ
    tok = lambda w: pl.BlockSpec((None, TQ, w), lambda b, i: (b, i, 0))
    const2 = lambda b, i: (0, 0)
    return pl.pallas_call(
        functools.partial(_attn_kernel, topk, S),
        grid=(B, n_q),
        in_specs=[
            tok(D),
            tok(CONV_WIDTH),
            tok(ATTN_WIDTH),
            tok(ATTN_WIDTH),
            tok(IDX_HEADS * IDX_DIM),
            pl.BlockSpec((None, IDX_HEADS, TQ), lambda b, i: (b, 0, i)),
            pl.BlockSpec((None, S, ATTN_WIDTH), lambda b, i: (b, 0, 0)),
            pl.BlockSpec((None, S // CK, ATTN_WIDTH, CK), lambda b, i: (b, 0, 0, 0)),
            pl.BlockSpec((None, S, 2 * IDX_DIM), lambda b, i: (b, 0, 0)),
            pl.BlockSpec((None, 1, D), lambda b, i: (b, 0, 0)),
            pl.BlockSpec((D, D), const2),
            pl.BlockSpec((1, ATTN_WIDTH), const2),
            pl.BlockSpec((1, D), const2),
        ],
        out_specs=tok(D),
        out_shape=jax.ShapeDtypeStruct((B, S, D), F32),
        scratch_shapes=[
            pltpu.VMEM((S, TQ), I32),
            pltpu.VMEM((S, TQ), F32),
            pltpu.VMEM((ATTN_WIDTH, TQ), F32),
            pltpu.VMEM((ATTN_HEADS, TQ), F32),
            pltpu.VMEM((ATTN_HEADS, TQ), F32),
            pltpu.VMEM((N_PAIRS, 2 * TQ, LANES), BF16),
            pltpu.VMEM((N_PAIRS, 2 * TQ, LANES), BF16),
        ],
        compiler_params=pltpu.CompilerParams(
            dimension_semantics=("arbitrary", "arbitrary"),
            vmem_limit_bytes=VMEM_LIMIT_BYTES),
        name="dsa_attn_out",
    )(x, yc, gz, q, qi, wit, k, vt, ki2, gate, wout, gattn, gfin)


def kernel(x, c, w_ada, b_ada, w_in, conv_w, conv_b, idx_k_gain, idx_k_bias,
           mix_norm_gain, w_out, final_gain):
    B, S, D = x.shape
    assert w_ada.shape[0] == 1 and B <= SUBLANES
    assert D == D_MODEL and S % TM == 0 and TM == CK and CK % TQ == 0 and CK == 2 ** LOG2_CK
    c_pad = jnp.zeros((SUBLANES, D), F32).at[:B].set(c)
    grp = jnp.arange(CONV_WIDTH) // GROUP_DIM
    gmat = jnp.where(grp[:, None] == grp[None, :], 1.0 / GROUP_DIM, 0.0).astype(BF16)
    col = lambda w, a, n: w[:, a:a + n]
    o_q = 4 * CONV_WIDTH
    o_k, o_v, o_az = o_q + ATTN_WIDTH, o_q + 2 * ATTN_WIDTH, o_q + 3 * ATTN_WIDTH
    o_qi = o_q + 4 * ATTN_WIDTH
    o_ki = o_qi + IDX_HEADS * IDX_DIM
    o_wi = o_ki + IDX_DIM
    wl = w_in[0]
    w1 = jnp.concatenate(
        [col(wl, 0, 4 * CONV_WIDTH), col(wl, o_q, ATTN_WIDTH), col(wl, o_k, ATTN_WIDTH),
         col(wl, o_az, ATTN_WIDTH), col(wl, o_qi, IDX_HEADS * IDX_DIM),
         col(wl, o_ki, IDX_DIM), col(wl, o_ki, IDX_DIM)], axis=1).astype(BF16)
    w2t = jnp.concatenate(
        [col(wl, o_v, ATTN_WIDTH).T, col(wl, o_wi, IDX_HEADS).T,
         jnp.zeros((_W2_ROWS - ATTN_WIDTH - IDX_HEADS, D), F32)], axis=0).astype(BF16)
    mod = _modulation(c_pad, w_ada[0], b_ada[0][None, :])[:B]
    mod3 = mod.reshape(B, 3, D)
    dup = lambda v: jnp.concatenate([v, v])[None, :]
    yc, q, k, gz, qi, ki2, vt, wit = _projection(
        x, mod3, w1, w2t, conv_w[0], conv_b[0][None, :],
        dup(idx_k_gain[0]), dup(idx_k_bias[0]),
        mix_norm_gain[0][None, :CONV_WIDTH], gmat)
    return _attention(x, yc, gz, q, qi, wit, k, vt, ki2, mod3[:, 2:3, :],
                      w_out[0].astype(BF16), mix_norm_gain[0][None, CONV_WIDTH:],
                      final_gain[None, :])
```

```python
import functools

import numpy as np
import jax
import jax.numpy as jnp
from jax import lax
from jax.experimental import pallas as pl
from jax.experimental.pallas import tpu as pltpu

F32 = jnp.float32
BF16 = jnp.bfloat16
I32 = jnp.int32
I16 = jnp.int16

D_MODEL = 1024
CONV_WIDTH = 512
CONV_K = 3
ATTN_HEADS = 8
HEAD_DIM = 64
ATTN_WIDTH = ATTN_HEADS * HEAD_DIM
IDX_HEADS = 8
IDX_DIM = 64
TOPK_MAX = 256
GROUP_DIM = 64
NORM_EPS = 1e-6
NEG_INF = -1e30

LANES = 128
SUBLANES = 8
PACK16 = 2 * SUBLANES
VMEM_LIMIT_BYTES = 56 * 1024 * 1024

TM = 512
TQ = 128
CK = 512
LOG2_CK = 9
N_PAIRS = ATTN_HEADS // 2

INT_MIN = -(2 ** 31)
I16_MIN = -(2 ** 15)
_NEG_BITS = int(np.float32(NEG_INF).view(np.int32))
KEY_NEG = _NEG_BITS ^ 0x7FFFFFFF
KEY_NEG_HI = KEY_NEG >> 16
KEY_NEG_LO = (KEY_NEG & 0xFFFF) + I16_MIN

NT_DIMS = (((1,), (1,)), ((), ()))


def _alibi_slopes(n_heads):
    return [2.0 ** (-8.0 * (i + 1) / n_heads) for i in range(n_heads)]


def _mod_kernel(c_ref, w_ref, b_ref, o_ref):
    c_act = jax.nn.silu(c_ref[...])
    o_ref[...] = jnp.dot(c_act.astype(BF16), w_ref[...].astype(BF16),
                         preferred_element_type=F32) + b_ref[...]


def _modulation(c_pad, w_ada, b_ada):
    rows, d = c_pad.shape
    n_out = w_ada.shape[1]
    return pl.pallas_call(
        _mod_kernel,
        grid=(n_out // d,),
        in_specs=[
            pl.BlockSpec((rows, d), lambda j: (0, 0)),
            pl.BlockSpec((d, d), lambda j: (0, j)),
            pl.BlockSpec((1, d), lambda j: (0, j)),
        ],
        out_specs=pl.BlockSpec((rows, d), lambda j: (0, j)),
        out_shape=jax.ShapeDtypeStruct((rows, n_out), F32),
        compiler_params=pltpu.CompilerParams(
            dimension_semantics=("arbitrary",), vmem_limit_bytes=VMEM_LIMIT_BYTES),
        name="adaln_mod",
    )(c_pad, w_ada, b_ada)


_C_CONV = 0
_C_Q = 4 * CONV_WIDTH
_C_K = _C_Q + ATTN_WIDTH
_C_AZ = _C_K + ATTN_WIDTH
_C_QI = _C_AZ + ATTN_WIDTH
_C_KI = _C_QI + IDX_HEADS * IDX_DIM
_W1_COLS = _C_KI + 2 * IDX_DIM
_W2_ROWS = ATTN_WIDTH + 16


def _proj_kernel(x_ref, mod_ref, w1_ref, w2t_ref, cw_ref, cb_ref, lng_ref, lnb_ref,
                 gconv_ref, gmat_ref,
                 yc_ref, q_ref, k_ref, gz_ref, qi_ref, ki2_ref, vt_ref, wit_ref,
                 u_scr):
    i = pl.program_id(1)
    x = x_ref[...]
    ms = jnp.mean(x * x, axis=-1, keepdims=True)
    xn = x * lax.rsqrt(ms + NORM_EPS)
    shift = mod_ref[0:1, :]
    scale = mod_ref[1:2, :]
    h = (xn * (1.0 + scale) + shift).astype(BF16)

    def proj(c0, width):
        return jnp.dot(h, w1_ref[:, c0:c0 + width], preferred_element_type=F32)

    pc = proj(_C_CONV, 4 * CONV_WIDTH)
    b_gate = pc[:, 0:CONV_WIDTH]
    c_gate = pc[:, CONV_WIDTH:2 * CONV_WIDTH]
    x_in = pc[:, 2 * CONV_WIDTH:3 * CONV_WIDTH]
    z_conv = pc[:, 3 * CONV_WIDTH:4 * CONV_WIDTH]
    u = c_gate * x_in

    @pl.when(i == 0)
    def _():
        u_scr[0:SUBLANES, :] = jnp.zeros((SUBLANES, CONV_WIDTH), F32)

    u_scr[SUBLANES:SUBLANES + TM, :] = u
    u_m1 = u_scr[SUBLANES - 1:SUBLANES - 1 + TM, :]
    u_m2 = u_scr[SUBLANES - 2:SUBLANES - 2 + TM, :]
    y = cb_ref[...] + cw_ref[0:1, :] * u_m2
    y = y + cw_ref[1:2, :] * u_m1
    y = y + cw_ref[2:3, :] * u
    conv_out = b_gate * y
    u_scr[0:SUBLANES, :] = u_scr[TM:TM + SUBLANES, :]

    sq = conv_out * conv_out
    sq_hi = sq.astype(BF16)
    sq_lo = (sq - sq_hi.astype(F32)).astype(BF16)
    gms = (jnp.dot(sq_hi, gmat_ref[...], preferred_element_type=F32)
           + jnp.dot(sq_lo, gmat_ref[...], preferred_element_type=F32))
    yc = conv_out * lax.rsqrt(gms + NORM_EPS) * gconv_ref[...] * jax.nn.silu(z_conv)
    yc_ref[...] = yc.astype(BF16)

    q_ref[...] = (proj(_C_Q, ATTN_WIDTH) * (HEAD_DIM ** -0.5)).astype(BF16)
    k_ref[...] = proj(_C_K, ATTN_WIDTH).astype(BF16)
    gz_ref[...] = jax.nn.silu(proj(_C_AZ, ATTN_WIDTH))
    qi_ref[...] = proj(_C_QI, IDX_HEADS * IDX_DIM).astype(BF16)

    pki = proj(_C_KI, 2 * IDX_DIM)
    mu = jnp.mean(pki, axis=-1, keepdims=True)
    var = jnp.mean(jnp.square(pki - mu), axis=-1, keepdims=True)
    ki = (pki - mu) * lax.rsqrt(var + NORM_EPS) * lng_ref[...] + lnb_ref[...]
    ki2_ref[...] = ki.astype(BF16)

    pt = lax.dot_general(w2t_ref[...], h, NT_DIMS, preferred_element_type=F32)
    vt_ref[...] = pt[0:ATTN_WIDTH, :].astype(BF16)
    idx_scale = (IDX_DIM ** -0.5) * (IDX_HEADS ** -0.5)
    wit_ref[...] = pt[ATTN_WIDTH:ATTN_WIDTH + IDX_HEADS, :] * idx_scale


def _projection(x, mod3, w1, w2t, conv_w, conv_b, lng2, lnb2, gconv, gmat):
    B, S, D = x.shape
    n_t = S // TM
    const2 = lambda b, i: (0, 0)
    out_shapes = (
        jax.ShapeDtypeStruct((B, S, CONV_WIDTH), BF16),
        jax.ShapeDtypeStruct((B, S, ATTN_WIDTH), BF16),
        jax.ShapeDtypeStruct((B, S, ATTN_WIDTH), BF16),
        jax.ShapeDtypeStruct((B, S, ATTN_WIDTH), F32),
        jax.ShapeDtypeStruct((B, S, IDX_HEADS * IDX_DIM), BF16),
        jax.ShapeDtypeStruct((B, S, 2 * IDX_DIM), BF16),
        jax.ShapeDtypeStruct((B, n_t, ATTN_WIDTH, TM), BF16),
        jax.ShapeDtypeStruct((B, IDX_HEADS, S), F32),
    )
    tok = lambda w: pl.BlockSpec((None, TM, w), lambda b, i: (b, i, 0))
    return pl.pallas_call(
        _proj_kernel,
        grid=(B, n_t),
        in_specs=[
            pl.BlockSpec((None, TM, D), lambda b, i: (b, i, 0)),
            pl.BlockSpec((None, 3, D), lambda b, i: (b, 0, 0)),
            pl.BlockSpec((D, _W1_COLS), const2),
            pl.BlockSpec((_W2_ROWS, D), const2),
            pl.BlockSpec((CONV_K, CONV_WIDTH), const2),
            pl.BlockSpec((1, CONV_WIDTH), const2),
            pl.BlockSpec((1, 2 * IDX_DIM), const2),
            pl.BlockSpec((1, 2 * IDX_DIM), const2),
            pl.BlockSpec((1, CONV_WIDTH), const2),
            pl.BlockSpec((CONV_WIDTH, CONV_WIDTH), const2),
        ],
        out_specs=(
            tok(CONV_WIDTH), tok(ATTN_WIDTH), tok(ATTN_WIDTH), tok(ATTN_WIDTH),
            tok(IDX_HEADS * IDX_DIM), tok(2 * IDX_DIM),
            pl.BlockSpec((None, None, ATTN_WIDTH, TM), lambda b, i: (b, i, 0, 0)),
            pl.BlockSpec((None, IDX_HEADS, TM), lambda b, i: (b, 0, i)),
        ),
        out_shape=out_shapes,
        scratch_shapes=[pltpu.VMEM((TM + 2 * SUBLANES, CONV_WIDTH), F32)],
        compiler_params=pltpu.CompilerParams(
            dimension_semantics=("arbitrary", "arbitrary"),
            vmem_limit_bytes=VMEM_LIMIT_BYTES),
        name="proj_conv",
    )(x, mod3, w1, w2t, conv_w, conv_b, lng2, lnb2, gconv, gmat)


def _tree(op, xs):
    xs = list(xs)
    while len(xs) > 1:
        xs = [op(xs[i], xs[i + 1]) if i + 1 < len(xs) else xs[i]
              for i in range(0, len(xs), 2)]
    return xs[0]


def _fold_rows(op, x, rows):
    return _tree(op, [x[r:r + rows] for r in range(0, x.shape[0], rows)])


def _attn_kernel(topk, seq_len,
                 x_ref, yc_ref, gz_ref, q_ref, qi_ref, wit_ref, k_ref, vt_ref, ki2_ref,
                 gate_ref, wout_ref, gattn_ref, gfin_ref,
                 o_ref,
                 keys_scr, hi_scr, lom_scr, mb_scr, lg_scr, acc_scr, pos_scr, qiw_scr, qw_scr):
    qb = pl.program_id(1)
    t0 = qb * TQ
    n_chunks = lax.shift_right_logical(t0 + (TQ - 1), LOG2_CK) + 1
    slopes = _alibi_slopes(ATTN_HEADS)

    def chunk_loop(body, init):
        return lax.fori_loop(
            0, n_chunks, lambda c, carry: body(c, pl.multiple_of(c * CK, CK), carry), init)

    lane_k = lax.broadcasted_iota(I32, (CK, LANES), 1)
    row_k = lax.broadcasted_iota(I32, (CK, LANES), 0)

    @pl.when(qb == 0)
    def _():
        def body(c, carry):
            base = pl.multiple_of(c * CK, CK)
            s = base + row_k
            val = jnp.where(lane_k == 0, lax.shift_right_logical(s, 6),
                            jnp.where(lane_k == 1, s & 63,
                                      jnp.where(lane_k <= 3, 1, 0)))
            pos_scr[pl.ds(base, CK), :] = val.astype(F32).astype(BF16)
            return carry
        lax.fori_loop(0, seq_len // CK, body, 0)

    lane = lax.broadcasted_iota(I32, (TQ, LANES), 1)
    t_row = t0 + lax.broadcasted_iota(I32, (TQ, LANES), 0)
    t_hi = lax.shift_right_logical(t_row, 6).astype(F32)
    t_lo = (t_row & 63).astype(F32)
    lo_half = lane < HEAD_DIM
    for j in range(N_PAIRS):
        pair = qi_ref[:, j * LANES:(j + 1) * LANES]
        zero = jnp.zeros_like(pair)
        qiw_scr[j, 0:TQ, :] = jnp.where(lo_half, pair, zero)
        qiw_scr[j, TQ:2 * TQ, :] = jnp.where(lo_half, zero, pair)
        pair = q_ref[:, j * LANES:(j + 1) * LANES]
        for e in range(2):
            sl = slopes[2 * j + e]
            rows = slice(e * TQ, (e + 1) * TQ)
            keep = lo_half if e == 0 else jnp.logical_not(lo_half)
            qw_scr[j, rows, 0:LANES] = jnp.where(keep, pair, zero)
            pcol = jnp.where(lane == 0, 64.0 * sl,
                             jnp.where(lane == 1, sl,
                                       jnp.where(lane == 2, (-64.0 * sl) * t_hi,
                                                 jnp.where(lane == 3, (-sl) * t_lo, 0.0))))
            qw_scr[j, rows, LANES:2 * LANES] = pcol.astype(BF16)

    row = lax.broadcasted_iota(I32, (CK, TQ), 0)
    t_idx = t0 + lax.broadcasted_iota(I32, (CK, TQ), 1)

    def score_body(c, base, carry):
        kic = ki2_ref[pl.ds(base, CK), :]
        score = None
        for j in range(N_PAIRS):
            d = lax.dot_general(kic, qiw_scr[j], NT_DIMS, preferred_element_type=F32)
            r = jnp.maximum(d, 0.0)
            term = (r[:, 0:TQ] * wit_ref[2 * j:2 * j + 1, :]
                    + r[:, TQ:2 * TQ] * wit_ref[2 * j + 1:2 * j + 2, :])
            score = term if score is None else score + term
        bits = pltpu.bitcast(score, I32)
        key = jnp.where(bits >= 0, bits, bits ^ 0x7FFFFFFF)
        key = jnp.where(base + row <= t_idx, key, INT_MIN)
        keys_scr[pl.ds(base, CK), :] = key
        hi_scr[pl.ds(base, CK), :] = lax.shift_right_arithmetic(key, 16).astype(I16)
        return carry

    chunk_loop(score_body, 0)

    n_acc = 4
    one16 = jnp.ones((PACK16, TQ), I16)
    zero16 = jnp.zeros((PACK16, TQ), I16)
    zero_row = jnp.zeros((1, TQ), I32)

    def total16(accs):
        tot = ((accs[0] + accs[1]) + (accs[2] + accs[3])).astype(I32)
        return jnp.sum(tot, axis=0, keepdims=True)

    def count16(src, pred):
        def body(c, base, accs):
            accs = list(accs)
            blk = src[pl.ds(base, CK), :]
            for r in range(CK // PACK16):
                accs[r % n_acc] = accs[r % n_acc] + pred(blk[r * PACK16:(r + 1) * PACK16])
            return tuple(accs)
        return total16(chunk_loop(body, (zero16,) * n_acc))

    n_noncausal = (seq_len - 1) - (t0 + lax.broadcasted_iota(I32, (1, TQ), 1))

    def search16(src, target, extra, cnt_init):
        def bit_body(ib, carry):
            u_thr, cnt_thr = carry
            cand_u = u_thr | lax.shift_left(jnp.int32(1), 15 - ib)
            cand = cand_u + I16_MIN
            cand16 = jnp.broadcast_to(cand, (PACK16, TQ)).astype(I16)
            cnt = count16(src, lambda kv: jnp.where(kv >= cand16, one16, zero16))
            ok = cnt + extra(cand) >= target
            return jnp.where(ok, cand_u, u_thr), jnp.where(ok, cnt, cnt_thr)
        u_thr, cnt_thr = lax.fori_loop(0, 16, bit_body, (zero_row, cnt_init))
        return u_thr + I16_MIN, cnt_thr

    thr_hi, cnt_ge_hi = search16(
        hi_scr, topk, lambda v: jnp.where(v <= KEY_NEG_HI, n_noncausal, 0), zero_row)

    thr_hi16 = jnp.broadcast_to(thr_hi, (PACK16, TQ)).astype(I16)

    def split_body(c, base, accs):
        accs = list(accs)
        hblk = hi_scr[pl.ds(base, CK), :]
        lo_blk = ((keys_scr[pl.ds(base, CK), :] & 0xFFFF) + I16_MIN).astype(I16)
        min16 = jnp.full((PACK16, TQ), I16_MIN, I16)
        lom = []
        for r in range(CK // PACK16):
            rows = slice(r * PACK16, (r + 1) * PACK16)
            hv = hblk[rows]
            accs[r % n_acc] = accs[r % n_acc] + jnp.where(hv > thr_hi16, one16, zero16)
            lom.append(jnp.where(hv == thr_hi16, lo_blk[rows], min16))
        lom_scr[pl.ds(base, CK), :] = jnp.concatenate(lom, axis=0)
        return tuple(accs)

    cnt_gt_hi = total16(chunk_loop(split_body, (zero16,) * n_acc))
    need_lo = topk - (cnt_gt_hi + jnp.where(KEY_NEG_HI > thr_hi, n_noncausal, 0))
    neg_in_bucket = thr_hi == KEY_NEG_HI
    thr_lo, cnt_lo = search16(
        lom_scr, need_lo,
        lambda v: jnp.where(jnp.logical_and(neg_in_bucket, v <= KEY_NEG_LO), n_noncausal, 0),
        cnt_ge_hi - cnt_gt_hi)
    thr = thr_hi * 65536 + (thr_lo - I16_MIN)
    cnt_thr = cnt_gt_hi + cnt_lo

    def mask_fast():
        def body(c, base, carry):
            kv = keys_scr[pl.ds(base, CK), :]
            mb_scr[pl.ds(base, CK), :] = jnp.where(kv >= thr, 0.0, NEG_INF)
            return carry
        chunk_loop(body, 0)

    def mask_ties():
        sub = lax.broadcasted_iota(I32, (SUBLANES, TQ), 0)

        def count_rows(pred):
            def body(c, base, acc):
                for r in range(CK // SUBLANES):
                    kv = keys_scr[pl.ds(base + r * SUBLANES, SUBLANES), :]
                    acc = acc + pred(kv, base + r * SUBLANES)
                return acc
            acc = chunk_loop(body, jnp.zeros((SUBLANES, TQ), I32))
            return jnp.sum(acc, axis=0, keepdims=True)

        cnt_gt = count_rows(lambda kv, _: jnp.where(kv > thr, 1, 0))
        need = topk - cnt_gt
        n_bits = (seq_len - 1).bit_length()

        def idx_body(ib, j_thr):
            cand = j_thr | lax.shift_left(jnp.int32(1), (n_bits - 1) - ib)
            cnt = count_rows(lambda kv, r0: jnp.where(
                kv == thr, jnp.where(r0 + sub < cand, 1, 0), 0))
            return jnp.where(cnt < need, cand, j_thr)

        j_thr = lax.fori_loop(0, n_bits, idx_body, zero_row)

        def body(c, base, carry):
            kv = keys_scr[pl.ds(base, CK), :]
            tie_ok = jnp.where(base + row <= j_thr, 0.0, NEG_INF)
            mb_scr[pl.ds(base, CK), :] = jnp.where(
                kv > thr, 0.0, jnp.where(kv == thr, tie_ok, NEG_INF))
            return carry
        chunk_loop(body, 0)

    lax.cond(jnp.max(cnt_thr) > topk, mask_ties, mask_fast)

    def logit_body(c, base, mx):
        mx = list(mx)
        mb = mb_scr[pl.ds(base, CK), :]
        pos = pos_scr[pl.ds(base, CK), :]
        for j in range(N_PAIRS):
            lhs = jnp.concatenate([k_ref[pl.ds(base, CK), j * LANES:(j + 1) * LANES], pos],
                                  axis=1)
            lg = lax.dot_general(lhs, qw_scr[j], NT_DIMS, preferred_element_type=F32)
            for e in range(2):
                hd = 2 * j + e
                logit = lg[:, e * TQ:(e + 1) * TQ] + mb
                lg_scr[hd, pl.ds(base, CK), :] = logit
                mx[hd] = jnp.maximum(mx[hd], _fold_rows(jnp.maximum, logit, SUBLANES))
        return tuple(mx)

    neg_slab = jnp.full((SUBLANES, TQ), NEG_INF, F32)
    mx = chunk_loop(logit_body, (neg_slab,) * ATTN_HEADS)
    m_fin = [jnp.max(v, axis=0, keepdims=True) for v in mx]

    acc_scr[...] = jnp.zeros((ATTN_WIDTH, TQ), F32)

    def pv_body(c, base, sm):
        sm = list(sm)
        for hd in range(ATTN_HEADS):
            p = jnp.exp(lg_scr[hd, pl.ds(base, CK), :] - m_fin[hd])
            sm[hd] = sm[hd] + _fold_rows(jnp.add, p, SUBLANES)
            rows = slice(hd * HEAD_DIM, (hd + 1) * HEAD_DIM)
            acc_scr[rows, :] += jnp.dot(vt_ref[c, rows, :], p.astype(BF16),
                                        preferred_element_type=F32)
        return tuple(sm)

    zero_slab = jnp.zeros((SUBLANES, TQ), F32)
    sm = chunk_loop(pv_body, (zero_slab,) * ATTN_HEADS)

    normed = []
    for hd in range(ATTN_HEADS):
        rows = slice(hd * HEAD_DIM, (hd + 1) * HEAD_DIM)
        o = acc_scr[rows, :] / jnp.sum(sm[hd], axis=0, keepdims=True)
        ms = jnp.mean(o * o, axis=0, keepdims=True)
        normed.append(o * lax.rsqrt(ms + NORM_EPS))
    attn_t = jnp.concatenate(normed, axis=0)
    ya = attn_t.T * gattn_ref[...] * gz_ref[...]
    upd = (jnp.dot(yc_ref[...], wout_ref[0:CONV_WIDTH, :], preferred_element_type=F32)
           + jnp.dot(ya.astype(BF16), wout_ref[CONV_WIDTH:, :],
                     preferred_element_type=F32))
    xr = x_ref[...] + gate_ref[...] * upd
    ms = jnp.mean(xr * xr, axis=-1, keepdims=True)
    o_ref[...] = xr * lax.rsqrt(ms + NORM_EPS) * gfin_ref[...]


def _attention(x, yc, gz, q, qi, wit, k, vt, ki2, gate, wout, gattn, gfin):
    B, S, D = x.shape
    topk = min(TOPK_MAX, S // 4)
    n_q = S // TQ
    tok = lambda w: pl.BlockSpec((None, TQ, w), lambda b, i: (b, i, 0))
    const2 = lambda b, i: (0, 0)
    once = dict(pipeline_mode=pl.Buffered(1))
    return pl.pallas_call(
        functools.partial(_attn_kernel, topk, S),
        grid=(B, n_q),
        in_specs=[
            tok(D),
            tok(CONV_WIDTH),
            tok(ATTN_WIDTH),
            tok(ATTN_WIDTH),
            tok(IDX_HEADS * IDX_DIM),
            pl.BlockSpec((None, IDX_HEADS, TQ), lambda b, i: (b, 0, i)),
            pl.BlockSpec((None, S, ATTN_WIDTH), lambda b, i: (b, 0, 0), **once),
            pl.BlockSpec((None, S // CK, ATTN_WIDTH, CK), lambda b, i: (b, 0, 0, 0),
                         **once),
            pl.BlockSpec((None, S, 2 * IDX_DIM), lambda b, i: (b, 0, 0), **once),
            pl.BlockSpec((None, 1, D), lambda b, i: (b, 0, 0)),
            pl.BlockSpec((D, D), const2, **once),
            pl.BlockSpec((1, ATTN_WIDTH), const2),
            pl.BlockSpec((1, D), const2),
        ],
        out_specs=tok(D),
        out_shape=jax.ShapeDtypeStruct((B, S, D), F32),
        scratch_shapes=[
            pltpu.VMEM((S, TQ), I32),
            pltpu.VMEM((S, TQ), I16),
            pltpu.VMEM((S, TQ), I16),
            pltpu.VMEM((S, TQ), F32),
            pltpu.VMEM((ATTN_HEADS, S, TQ), F32),
            pltpu.VMEM((ATTN_WIDTH, TQ), F32),
            pltpu.VMEM((S, LANES), BF16),
            pltpu.VMEM((N_PAIRS, 2 * TQ, LANES), BF16),
            pltpu.VMEM((N_PAIRS, 2 * TQ, 2 * LANES), BF16),
        ],
        compiler_params=pltpu.CompilerParams(
            dimension_semantics=("arbitrary", "arbitrary"),
            vmem_limit_bytes=VMEM_LIMIT_BYTES),
        name="dsa_attn_out",
    )(x, yc, gz, q, qi, wit, k, vt, ki2, gate, wout, gattn, gfin)


def kernel(x, c, w_ada, b_ada, w_in, conv_w, conv_b, idx_k_gain, idx_k_bias,
           mix_norm_gain, w_out, final_gain):
    B, S, D = x.shape
    assert w_ada.shape[0] == 1 and B <= SUBLANES
    assert D == D_MODEL and S % TM == 0 and TM == CK and CK % TQ == 0 and CK == 2 ** LOG2_CK
    c_pad = jnp.zeros((SUBLANES, D), F32).at[:B].set(c)
    grp = jnp.arange(CONV_WIDTH) // GROUP_DIM
    gmat = jnp.where(grp[:, None] == grp[None, :], 1.0 / GROUP_DIM, 0.0).astype(BF16)
    col = lambda w, a, n: w[:, a:a + n]
    o_q = 4 * CONV_WIDTH
    o_k, o_v, o_az = o_q + ATTN_WIDTH, o_q + 2 * ATTN_WIDTH, o_q + 3 * ATTN_WIDTH
    o_qi = o_q + 4 * ATTN_WIDTH
    o_ki = o_qi + IDX_HEADS * IDX_DIM
    o_wi = o_ki + IDX_DIM
    wl = w_in[0]
    w1 = jnp.concatenate(
        [col(wl, 0, 4 * CONV_WIDTH), col(wl, o_q, ATTN_WIDTH), col(wl, o_k, ATTN_WIDTH),
         col(wl, o_az, ATTN_WIDTH), col(wl, o_qi, IDX_HEADS * IDX_DIM),
         col(wl, o_ki, IDX_DIM), col(wl, o_ki, IDX_DIM)], axis=1).astype(BF16)
    w2t = jnp.concatenate(
        [col(wl, o_v, ATTN_WIDTH).T, col(wl, o_wi, IDX_HEADS).T,
         jnp.zeros((_W2_ROWS - ATTN_WIDTH - IDX_HEADS, D), F32)], axis=0).astype(BF16)
    mod = _modulation(c_pad, w_ada[0], b_ada[0][None, :])[:B]
    mod3 = mod.reshape(B, 3, D)
    dup = lambda v: jnp.concatenate([v, v])[None, :]
    yc, q, k, gz, qi, ki2, vt, wit = _projection(
        x, mod3, w1, w2t, conv_w[0], conv_b[0][None, :],
        dup(idx_k_gain[0]), dup(idx_k_bias[0]),
        mix_norm_gain[0][None, :CONV_WIDTH], gmat)
    return _attention(x, yc, gz, q, qi, wit, k, vt, ki2, mod3[:, 2:3, :],
                      w_out[0].astype(BF16), mix_norm_gain[0][None, CONV_WIDTH:],
                      final_gain[None, :])
```

```python
import functools

import numpy as np
import jax
import jax.numpy as jnp
from jax import lax
from jax.experimental import pallas as pl
from jax.experimental.pallas import tpu as pltpu

F32 = jnp.float32
BF16 = jnp.bfloat16
I32 = jnp.int32
I16 = jnp.int16

D_MODEL = 1024
CONV_WIDTH = 512
CONV_K = 3
ATTN_HEADS = 8
HEAD_DIM = 64
ATTN_WIDTH = ATTN_HEADS * HEAD_DIM
IDX_HEADS = 8
IDX_DIM = 64
TOPK_MAX = 256
GROUP_DIM = 64
NORM_EPS = 1e-6
NEG_INF = -1e30

LANES = 128
SUBLANES = 8
PACK16 = 2 * SUBLANES
VMEM_LIMIT_BYTES = 56 * 1024 * 1024

TM = 512
TQ = 256
CK = 512
LOG2_CK = 9
N_PAIRS = ATTN_HEADS // 2

I16_MIN = -(2 ** 15)
NEG_INF_BF16 = float(np.asarray(NEG_INF, dtype=BF16).astype(np.float32))

NT_DIMS = (((1,), (1,)), ((), ()))


def _alibi_slopes(n_heads):
    return [2.0 ** (-8.0 * (i + 1) / n_heads) for i in range(n_heads)]


def _mod_kernel(c_ref, w_ref, b_ref, o_ref):
    c_act = jax.nn.silu(c_ref[...])
    o_ref[...] = jnp.dot(c_act.astype(BF16), w_ref[...].astype(BF16),
                         preferred_element_type=F32) + b_ref[...]


def _modulation(c_pad, w_ada, b_ada):
    rows, d = c_pad.shape
    n_out = w_ada.shape[1]
    return pl.pallas_call(
        _mod_kernel,
        grid=(n_out // d,),
        in_specs=[
            pl.BlockSpec((rows, d), lambda j: (0, 0)),
            pl.BlockSpec((d, d), lambda j: (0, j)),
            pl.BlockSpec((1, d), lambda j: (0, j)),
        ],
        out_specs=pl.BlockSpec((rows, d), lambda j: (0, j)),
        out_shape=jax.ShapeDtypeStruct((rows, n_out), F32),
        compiler_params=pltpu.CompilerParams(
            dimension_semantics=("arbitrary",), vmem_limit_bytes=VMEM_LIMIT_BYTES),
        name="adaln_mod",
    )(c_pad, w_ada, b_ada)


_C_CONV = 0
_C_Q = 4 * CONV_WIDTH
_C_K = _C_Q + ATTN_WIDTH
_C_AZ = _C_K + ATTN_WIDTH
_C_QI = _C_AZ + ATTN_WIDTH
_C_KI = _C_QI + IDX_HEADS * IDX_DIM
_W1_COLS = _C_KI + 2 * IDX_DIM
_W2_ROWS = ATTN_WIDTH + 16


def _proj_kernel(x_ref, mod_ref, w1_ref, w2t_ref, cw_ref, cb_ref, lng_ref, lnb_ref,
                 gconv_ref, gmat_ref,
                 yc_ref, q_ref, k_ref, gz_ref, qi_ref, ki2_ref, vt_ref, wit_ref,
                 u_scr):
    i = pl.program_id(1)
    x = x_ref[...]
    ms = jnp.mean(x * x, axis=-1, keepdims=True)
    xn = x * lax.rsqrt(ms + NORM_EPS)
    shift = mod_ref[0:1, :]
    scale = mod_ref[1:2, :]
    h = (xn * (1.0 + scale) + shift).astype(BF16)

    def proj(c0, width):
        return jnp.dot(h, w1_ref[:, c0:c0 + width], preferred_element_type=F32)

    pc = proj(_C_CONV, 4 * CONV_WIDTH)
    b_gate = pc[:, 0:CONV_WIDTH]
    c_gate = pc[:, CONV_WIDTH:2 * CONV_WIDTH]
    x_in = pc[:, 2 * CONV_WIDTH:3 * CONV_WIDTH]
    z_conv = pc[:, 3 * CONV_WIDTH:4 * CONV_WIDTH]
    u = c_gate * x_in

    @pl.when(i == 0)
    def _():
        u_scr[0:SUBLANES, :] = jnp.zeros((SUBLANES, CONV_WIDTH), F32)

    u_scr[SUBLANES:SUBLANES + TM, :] = u
    u_m1 = u_scr[SUBLANES - 1:SUBLANES - 1 + TM, :]
    u_m2 = u_scr[SUBLANES - 2:SUBLANES - 2 + TM, :]
    y = cb_ref[...] + cw_ref[0:1, :] * u_m2
    y = y + cw_ref[1:2, :] * u_m1
    y = y + cw_ref[2:3, :] * u
    conv_out = b_gate * y
    u_scr[0:SUBLANES, :] = u_scr[TM:TM + SUBLANES, :]

    sq = conv_out * conv_out
    sq_hi = sq.astype(BF16)
    sq_lo = (sq - sq_hi.astype(F32)).astype(BF16)
    gms = (jnp.dot(sq_hi, gmat_ref[...], preferred_element_type=F32)
           + jnp.dot(sq_lo, gmat_ref[...], preferred_element_type=F32))
    yc = conv_out * lax.rsqrt(gms + NORM_EPS) * gconv_ref[...] * jax.nn.silu(z_conv)
    yc_ref[...] = yc.astype(BF16)

    q_ref[...] = (proj(_C_Q, ATTN_WIDTH) * (HEAD_DIM ** -0.5)).astype(BF16)
    k_ref[...] = proj(_C_K, ATTN_WIDTH).astype(BF16)
    gz_ref[...] = jax.nn.silu(proj(_C_AZ, ATTN_WIDTH))
    qi_ref[...] = proj(_C_QI, IDX_HEADS * IDX_DIM).astype(BF16)

    pki = proj(_C_KI, 2 * IDX_DIM)
    mu = jnp.mean(pki, axis=-1, keepdims=True)
    var = jnp.mean(jnp.square(pki - mu), axis=-1, keepdims=True)
    ki = (pki - mu) * lax.rsqrt(var + NORM_EPS) * lng_ref[...] + lnb_ref[...]
    ki2_ref[...] = ki.astype(BF16)

    pt = lax.dot_general(w2t_ref[...], h, NT_DIMS, preferred_element_type=F32)
    vt_ref[...] = pt[0:ATTN_WIDTH, :].astype(BF16)
    idx_scale = (IDX_DIM ** -0.5) * (IDX_HEADS ** -0.5)
    wit_ref[...] = pt[ATTN_WIDTH:ATTN_WIDTH + IDX_HEADS, :] * idx_scale


def _projection(x, mod3, w1, w2t, conv_w, conv_b, lng2, lnb2, gconv, gmat):
    B, S, D = x.shape
    n_t = S // TM
    const2 = lambda b, i: (0, 0)
    out_shapes = (
        jax.ShapeDtypeStruct((B, S, CONV_WIDTH), BF16),
        jax.ShapeDtypeStruct((B, S, ATTN_WIDTH), BF16),
        jax.ShapeDtypeStruct((B, S, ATTN_WIDTH), BF16),
        jax.ShapeDtypeStruct((B, S, ATTN_WIDTH), F32),
        jax.ShapeDtypeStruct((B, S, IDX_HEADS * IDX_DIM), BF16),
        jax.ShapeDtypeStruct((B, S, 2 * IDX_DIM), BF16),
        jax.ShapeDtypeStruct((B, n_t, ATTN_WIDTH, TM), BF16),
        jax.ShapeDtypeStruct((B, IDX_HEADS, S), F32),
    )
    tok = lambda w: pl.BlockSpec((None, TM, w), lambda b, i: (b, i, 0))
    return pl.pallas_call(
        _proj_kernel,
        grid=(B, n_t),
        in_specs=[
            pl.BlockSpec((None, TM, D), lambda b, i: (b, i, 0)),
            pl.BlockSpec((None, 3, D), lambda b, i: (b, 0, 0)),
            pl.BlockSpec((D, _W1_COLS), const2),
            pl.BlockSpec((_W2_ROWS, D), const2),
            pl.BlockSpec((CONV_K, CONV_WIDTH), const2),
            pl.BlockSpec((1, CONV_WIDTH), const2),
            pl.BlockSpec((1, 2 * IDX_DIM), const2),
            pl.BlockSpec((1, 2 * IDX_DIM), const2),
            pl.BlockSpec((1, CONV_WIDTH), const2),
            pl.BlockSpec((CONV_WIDTH, CONV_WIDTH), const2),
        ],
        out_specs=(
            tok(CONV_WIDTH), tok(ATTN_WIDTH), tok(ATTN_WIDTH), tok(ATTN_WIDTH),
            tok(IDX_HEADS * IDX_DIM), tok(2 * IDX_DIM),
            pl.BlockSpec((None, None, ATTN_WIDTH, TM), lambda b, i: (b, i, 0, 0)),
            pl.BlockSpec((None, IDX_HEADS, TM), lambda b, i: (b, 0, i)),
        ),
        out_shape=out_shapes,
        scratch_shapes=[pltpu.VMEM((TM + 2 * SUBLANES, CONV_WIDTH), F32)],
        compiler_params=pltpu.CompilerParams(
            dimension_semantics=("arbitrary", "arbitrary"),
            vmem_limit_bytes=VMEM_LIMIT_BYTES),
        name="proj_conv",
    )(x, mod3, w1, w2t, conv_w, conv_b, lng2, lnb2, gconv, gmat)


def _tree(op, xs):
    xs = list(xs)
    while len(xs) > 1:
        xs = [op(xs[i], xs[i + 1]) if i + 1 < len(xs) else xs[i]
              for i in range(0, len(xs), 2)]
    return xs[0]


def _fold_rows(op, x, rows):
    return _tree(op, [x[r:r + rows] for r in range(0, x.shape[0], rows)])


def _attn_kernel(topk, seq_len,
                 x_ref, yc_ref, gz_ref, q_ref, qi_ref, wit_ref, qin_ref, witn_ref,
                 k_ref, vt_ref, ki2_ref, gate_ref, wout_ref, gattn_ref, gfin_ref,
                 o_ref,
                 sc_scr, sb_scr, mb_scr, lga_scr, lgb_scr, acc_scr, m_scr, l_scr, pos_scr,
                 qiw_scr, qw_scr):
    qb = pl.program_id(1)
    t0 = qb * TQ
    n_chunks = lax.shift_right_logical(t0 + (TQ - 1), LOG2_CK) + 1
    slopes = _alibi_slopes(ATTN_HEADS)

    def chunk_loop(body, init):
        return lax.fori_loop(
            0, n_chunks, lambda c, carry: body(c, pl.multiple_of(c * CK, CK), carry), init)

    lane_k = lax.broadcasted_iota(I32, (CK, LANES), 1)
    row_k = lax.broadcasted_iota(I32, (CK, LANES), 0)

    @pl.when(qb == 0)
    def _():
        def body(c, carry):
            base = pl.multiple_of(c * CK, CK)
            s = base + row_k
            val = jnp.where(lane_k == 0, lax.shift_right_logical(s, 6),
                            jnp.where(lane_k == 1, s & 63,
                                      jnp.where(lane_k <= 3, 1, 0)))
            pos_scr[pl.ds(base, CK), :] = val.astype(F32).astype(BF16)
            return carry
        lax.fori_loop(0, seq_len // CK, body, 0)

    lane = lax.broadcasted_iota(I32, (TQ, LANES), 1)
    t_row = t0 + lax.broadcasted_iota(I32, (TQ, LANES), 0)
    t_hi = lax.shift_right_logical(t_row, 6).astype(F32)
    t_lo = (t_row & 63).astype(F32)
    lo_half = lane < HEAD_DIM
    for j in range(N_PAIRS):
        pair = q_ref[:, j * LANES:(j + 1) * LANES]
        zero = jnp.zeros_like(pair)
        for e in range(2):
            sl = slopes[2 * j + e]
            rows = slice(e * TQ, (e + 1) * TQ)
            keep = lo_half if e == 0 else jnp.logical_not(lo_half)
            qw_scr[j, rows, 0:LANES] = jnp.where(keep, pair, zero)
            pcol = jnp.where(lane == 0, 64.0 * sl,
                             jnp.where(lane == 1, sl,
                                       jnp.where(lane == 2, (-64.0 * sl) * t_hi,
                                                 jnp.where(lane == 3, (-sl) * t_lo, 0.0))))
            qw_scr[j, rows, LANES:2 * LANES] = pcol.astype(BF16)

    row = lax.broadcasted_iota(I32, (CK, TQ), 0)
    q_lane = lax.broadcasted_iota(I32, (CK, TQ), 1)
    t_idx = t0 + q_lane

    def index_keys(qi_src, wit_src, blk):
        blk_t0 = blk * TQ
        for j in range(N_PAIRS):
            pair = qi_src[:, j * LANES:(j + 1) * LANES]
            zero = jnp.zeros_like(pair)
            qiw_scr[j, 0:TQ, :] = jnp.where(lo_half, pair, zero)
            qiw_scr[j, TQ:2 * TQ, :] = jnp.where(lo_half, zero, pair)

        def score_body(c, carry):
            base = pl.multiple_of(c * CK, CK)
            kic = ki2_ref[pl.ds(base, CK), :]
            score = None
            for j in range(N_PAIRS):
                d = lax.dot_general(kic, qiw_scr[j], NT_DIMS, preferred_element_type=F32)
                r = jnp.maximum(d, 0.0)
                term = (r[:, 0:TQ] * wit_src[2 * j:2 * j + 1, :]
                        + r[:, TQ:2 * TQ] * wit_src[2 * j + 1:2 * j + 2, :])
                score = term if score is None else score + term
            score = jnp.where(base + row <= blk_t0 + q_lane, score, -jnp.inf)
            sc_scr[pl.ds(base, CK), :] = score
            sb_scr[pl.ds(base, CK), :] = score.astype(BF16)
            return carry

        blk_chunks = lax.shift_right_logical(blk_t0 + (TQ - 1), LOG2_CK) + 1
        lax.fori_loop(0, blk_chunks, score_body, 0)

    @pl.when(qb == 0)
    def _():
        index_keys(qi_ref, wit_ref, qb)

    n_acc = 4
    one16 = jnp.ones((PACK16, TQ), I16)
    zero16 = jnp.zeros((PACK16, TQ), I16)
    zero_row = jnp.zeros((1, TQ), I32)
    n_noncausal = (seq_len - 1) - (t0 + lax.broadcasted_iota(I32, (1, TQ), 1))

    def key_to_f32(key):
        return pltpu.bitcast(jnp.where(key >= 0, key, key ^ 0x7FFFFFFF), F32)

    def count_bf16(cand):
        cand16 = jnp.broadcast_to(cand, (PACK16, TQ)).astype(BF16)

        def body(c, base, accs):
            accs = list(accs)
            blk = sb_scr[pl.ds(base, CK), :]
            for r in range(CK // PACK16):
                hit = blk[r * PACK16:(r + 1) * PACK16] >= cand16
                accs[r % n_acc] = accs[r % n_acc] + jnp.where(hit, one16, zero16)
            return tuple(accs)
        accs = chunk_loop(body, (zero16,) * n_acc)
        tot = ((accs[0] + accs[1]) + (accs[2] + accs[3])).astype(I32)
        return jnp.sum(tot, axis=0, keepdims=True)

    def count_f32(hit_fn):
        zero8 = jnp.zeros((SUBLANES, TQ), I32)

        def body(c, base, accs):
            accs = list(accs)
            blk = sc_scr[pl.ds(base, CK), :]
            for r in range(CK // SUBLANES):
                hit = hit_fn(blk[r * SUBLANES:(r + 1) * SUBLANES], base + r * SUBLANES)
                accs[r % n_acc] = accs[r % n_acc] + jnp.where(hit, 1, 0)
            return tuple(accs)
        accs = chunk_loop(body, (zero8,) * n_acc)
        return jnp.sum((accs[0] + accs[1]) + (accs[2] + accs[3]), axis=0, keepdims=True)

    def bf16_key(u16):
        k16 = u16 + I16_MIN
        return lax.shift_left(k16, 16) + jnp.where(k16 < 0, 0xFFFF, 0)

    def coarse_body(ib, u_thr):
        cand_u = u_thr | lax.shift_left(jnp.int32(1), 15 - ib)
        cand = key_to_f32(bf16_key(cand_u))
        total = count_bf16(cand) + jnp.where(cand <= NEG_INF_BF16, n_noncausal, 0)
        return jnp.where(total >= topk, cand_u, u_thr)

    u_coarse = lax.fori_loop(0, 16, coarse_body, zero_row)
    base_key = bf16_key(u_coarse) - 65536

    def fine_body(ib, off):
        cand_off = off | lax.shift_left(jnp.int32(1), 16 - ib)
        cand = key_to_f32(base_key + cand_off)
        total = (count_f32(lambda sv, _: sv >= cand)
                 + jnp.where(cand <= NEG_INF, n_noncausal, 0))
        return jnp.where(total >= topk, cand_off, off)

    thr = key_to_f32(base_key + lax.fori_loop(0, 17, fine_body, zero_row))
    cnt_thr = count_f32(lambda sv, _: sv >= thr)

    def mask_fast():
        def body(c, base, carry):
            kv = sc_scr[pl.ds(base, CK), :]
            mb_scr[pl.ds(base, CK), :] = jnp.where(kv >= thr, 0.0, NEG_INF)
            return carry
        chunk_loop(body, 0)

    def mask_ties():
        sub = lax.broadcasted_iota(I32, (SUBLANES, TQ), 0)
        need = topk - count_f32(lambda sv, _: sv > thr)
        n_bits = (seq_len - 1).bit_length()

        def idx_body(ib, j_thr):
            cand = j_thr | lax.shift_left(jnp.int32(1), (n_bits - 1) - ib)
            cnt = count_f32(lambda sv, r0: jnp.logical_and(sv == thr, r0 + sub < cand))
            return jnp.where(cnt < need, cand, j_thr)

        j_thr = lax.fori_loop(0, n_bits, idx_body, zero_row)

        def body(c, base, carry):
            kv = sc_scr[pl.ds(base, CK), :]
            tie_ok = jnp.where(base + row <= j_thr, 0.0, NEG_INF)
            mb_scr[pl.ds(base, CK), :] = jnp.where(
                kv > thr, 0.0, jnp.where(kv == thr, tie_ok, NEG_INF))
            return carry
        chunk_loop(body, 0)

    lax.cond(jnp.max(cnt_thr) > topk, mask_ties, mask_fast)

    def logits_chunk(c, dst):
        base = pl.multiple_of(c * CK, CK)
        mb = mb_scr[pl.ds(base, CK), :]
        pos = pos_scr[pl.ds(base, CK), :]
        for j in range(N_PAIRS):
            lhs = jnp.concatenate([k_ref[pl.ds(base, CK), j * LANES:(j + 1) * LANES], pos],
                                  axis=1)
            lg = lax.dot_general(lhs, qw_scr[j], NT_DIMS, preferred_element_type=F32)
            for e in range(2):
                dst[2 * j + e] = lg[:, e * TQ:(e + 1) * TQ] + mb

    def softmax_chunk(c, src):
        for hd in range(ATTN_HEADS):
            logit = src[hd]
            m_old = m_scr[hd:hd + 1, :]
            m_new = jnp.maximum(m_old, jnp.max(_fold_rows(jnp.maximum, logit, SUBLANES),
                                               axis=0, keepdims=True))
            alpha = jnp.exp(m_old - m_new)
            p = jnp.exp(logit - m_new)
            l_scr[hd:hd + 1, :] = alpha * l_scr[hd:hd + 1, :] + jnp.sum(
                _fold_rows(jnp.add, p, SUBLANES), axis=0, keepdims=True)
            m_scr[hd:hd + 1, :] = m_new
            rows = slice(hd * HEAD_DIM, (hd + 1) * HEAD_DIM)
            pv = jnp.dot(vt_ref[c, rows, :], p.astype(BF16), preferred_element_type=F32)
            acc_scr[rows, :] = alpha * acc_scr[rows, :] + pv

    m_scr[...] = jnp.full((ATTN_HEADS, TQ), NEG_INF, F32)
    l_scr[...] = jnp.zeros((ATTN_HEADS, TQ), F32)
    acc_scr[...] = jnp.zeros((ATTN_WIDTH, TQ), F32)
    logits_chunk(0, lga_scr)
    last = n_chunks - 1

    def attn_body(t, carry):
        c = 2 * t
        logits_chunk(jnp.minimum(c + 1, last), lgb_scr)
        softmax_chunk(c, lga_scr)

        @pl.when(c < last)
        def _():
            logits_chunk(jnp.minimum(c + 2, last), lga_scr)
            softmax_chunk(c + 1, lgb_scr)
        return carry

    lax.fori_loop(0, lax.shift_right_logical(n_chunks + 1, 1), attn_body, 0)

    normed = []
    for hd in range(ATTN_HEADS):
        rows = slice(hd * HEAD_DIM, (hd + 1) * HEAD_DIM)
        o = acc_scr[rows, :] / l_scr[hd:hd + 1, :]
        ms = jnp.mean(o * o, axis=0, keepdims=True)
        normed.append(o * lax.rsqrt(ms + NORM_EPS))
    attn_t = jnp.concatenate(normed, axis=0)
    ya = attn_t.T * gattn_ref[...] * gz_ref[...]
    upd = (jnp.dot(yc_ref[...], wout_ref[0:CONV_WIDTH, :], preferred_element_type=F32)
           + jnp.dot(ya.astype(BF16), wout_ref[CONV_WIDTH:, :],
                     preferred_element_type=F32))
    xr = x_ref[...] + gate_ref[...] * upd
    ms = jnp.mean(xr * xr, axis=-1, keepdims=True)
    o_ref[...] = xr * lax.rsqrt(ms + NORM_EPS) * gfin_ref[...]

    @pl.when(qb + 1 < pl.num_programs(1))
    def _():
        index_keys(qin_ref, witn_ref, qb + 1)


def _attention(x, yc, gz, q, qi, wit, k, vt, ki2, gate, wout, gattn, gfin):
    B, S, D = x.shape
    topk = min(TOPK_MAX, S // 4)
    n_q = S // TQ
    tok = lambda w: pl.BlockSpec((None, TQ, w), lambda b, i: (b, i, 0))
    const2 = lambda b, i: (0, 0)
    once = dict(pipeline_mode=pl.Buffered(1))
    nxt_tok = lambda b, i: (b, jnp.minimum(i + 1, n_q - 1), 0)
    nxt_row = lambda b, i: (b, 0, jnp.minimum(i + 1, n_q - 1))
    return pl.pallas_call(
        functools.partial(_attn_kernel, topk, S),
        grid=(B, n_q),
        in_specs=[
            tok(D),
            tok(CONV_WIDTH),
            tok(ATTN_WIDTH),
            tok(ATTN_WIDTH),
            tok(IDX_HEADS * IDX_DIM),
            pl.BlockSpec((None, IDX_HEADS, TQ), lambda b, i: (b, 0, i)),
            pl.BlockSpec((None, TQ, IDX_HEADS * IDX_DIM), nxt_tok),
            pl.BlockSpec((None, IDX_HEADS, TQ), nxt_row),
            pl.BlockSpec((None, S, ATTN_WIDTH), lambda b, i: (b, 0, 0), **once),
            pl.BlockSpec((None, S // CK, ATTN_WIDTH, CK), lambda b, i: (b, 0, 0, 0),
                         **once),
            pl.BlockSpec((None, S, 2 * IDX_DIM), lambda b, i: (b, 0, 0), **once),
            pl.BlockSpec((None, 1, D), lambda b, i: (b, 0, 0)),
            pl.BlockSpec((D, D), const2, **once),
            pl.BlockSpec((1, ATTN_WIDTH), const2),
            pl.BlockSpec((1, D), const2),
        ],
        out_specs=tok(D),
        out_shape=jax.ShapeDtypeStruct((B, S, D), F32),
        scratch_shapes=[
            pltpu.VMEM((S, TQ), F32),
            pltpu.VMEM((S, TQ), BF16),
            pltpu.VMEM((S, TQ), F32),
            pltpu.VMEM((ATTN_HEADS, CK, TQ), F32),
            pltpu.VMEM((ATTN_HEADS, CK, TQ), F32),
            pltpu.VMEM((ATTN_WIDTH, TQ), F32),
            pltpu.VMEM((ATTN_HEADS, TQ), F32),
            pltpu.VMEM((ATTN_HEADS, TQ), F32),
            pltpu.VMEM((S, LANES), BF16),
            pltpu.VMEM((N_PAIRS, 2 * TQ, LANES), BF16),
            pltpu.VMEM((N_PAIRS, 2 * TQ, 2 * LANES), BF16),
        ],
        compiler_params=pltpu.CompilerParams(
            dimension_semantics=("arbitrary", "arbitrary"),
            vmem_limit_bytes=VMEM_LIMIT_BYTES),
        name="dsa_attn_out",
    )(x, yc, gz, q, qi, wit, qi, wit, k, vt, ki2, gate, wout, gattn, gfin)


def kernel(x, c, w_ada, b_ada, w_in, conv_w, conv_b, idx_k_gain, idx_k_bias,
           mix_norm_gain, w_out, final_gain):
    B, S, D = x.shape
    assert w_ada.shape[0] == 1 and B <= SUBLANES
    assert D == D_MODEL and S % TM == 0 and TM == CK and CK % TQ == 0 and CK == 2 ** LOG2_CK
    c_pad = jnp.zeros((SUBLANES, D), F32).at[:B].set(c)
    grp = jnp.arange(CONV_WIDTH) // GROUP_DIM
    gmat = jnp.where(grp[:, None] == grp[None, :], 1.0 / GROUP_DIM, 0.0).astype(BF16)
    col = lambda w, a, n: w[:, a:a + n]
    o_q = 4 * CONV_WIDTH
    o_k, o_v, o_az = o_q + ATTN_WIDTH, o_q + 2 * ATTN_WIDTH, o_q + 3 * ATTN_WIDTH
    o_qi = o_q + 4 * ATTN_WIDTH
    o_ki = o_qi + IDX_HEADS * IDX_DIM
    o_wi = o_ki + IDX_DIM
    wl = w_in[0]
    w1 = jnp.concatenate(
        [col(wl, 0, 4 * CONV_WIDTH), col(wl, o_q, ATTN_WIDTH), col(wl, o_k, ATTN_WIDTH),
         col(wl, o_az, ATTN_WIDTH), col(wl, o_qi, IDX_HEADS * IDX_DIM),
         col(wl, o_ki, IDX_DIM), col(wl, o_ki, IDX_DIM)], axis=1).astype(BF16)
    w2t = jnp.concatenate(
        [col(wl, o_v, ATTN_WIDTH).T, col(wl, o_wi, IDX_HEADS).T,
         jnp.zeros((_W2_ROWS - ATTN_WIDTH - IDX_HEADS, D), F32)], axis=0).astype(BF16)
    mod = _modulation(c_pad, w_ada[0], b_ada[0][None, :])[:B]
    mod3 = mod.reshape(B, 3, D)
    dup = lambda v: jnp.concatenate([v, v])[None, :]
    yc, q, k, gz, qi, ki2, vt, wit = _projection(
        x, mod3, w1, w2t, conv_w[0], conv_b[0][None, :],
        dup(idx_k_gain[0]), dup(idx_k_bias[0]),
        mix_norm_gain[0][None, :CONV_WIDTH], gmat)
    return _attention(x, yc, gz, q, qi, wit, k, vt, ki2, mod3[:, 2:3, :],
                      w_out[0].astype(BF16), mix_norm_gain[0][None, CONV_WIDTH:],
                      final_gain[None, :])
```

```python
import functools

import numpy as np
import jax
import jax.numpy as jnp
from jax import lax
from jax.experimental import pallas as pl
from jax.experimental.pallas import tpu as pltpu

F32 = jnp.float32
BF16 = jnp.bfloat16
I32 = jnp.int32
I16 = jnp.int16

D_MODEL = 1024
CONV_WIDTH = 512
CONV_K = 3
ATTN_HEADS = 8
HEAD_DIM = 64
ATTN_WIDTH = ATTN_HEADS * HEAD_DIM
IDX_HEADS = 8
IDX_DIM = 64
TOPK_MAX = 256
GROUP_DIM = 64
NORM_EPS = 1e-6
NEG_INF = -1e30

LANES = 128
SUBLANES = 8
PACK16 = 2 * SUBLANES
VT_ROWS = HEAD_DIM + PACK16
VMEM_LIMIT_BYTES = 56 * 1024 * 1024

TM = 512
TQ = 256
CK = 512
LOG2_CK = 9
KT = 256
N_PAIRS = ATTN_HEADS // 2

I16_MIN = -(2 ** 15)
NEG_INF_BF16 = float(np.asarray(NEG_INF, dtype=BF16).astype(np.float32))

NT_DIMS = (((1,), (1,)), ((), ()))


def _alibi_slopes(n_heads):
    return [2.0 ** (-8.0 * (i + 1) / n_heads) for i in range(n_heads)]


def _mod_kernel(c_ref, w_ref, b_ref, o_ref):
    c_act = jax.nn.silu(c_ref[...])
    o_ref[...] = jnp.dot(c_act.astype(BF16), w_ref[...].astype(BF16),
                         preferred_element_type=F32) + b_ref[...]


def _modulation(c_pad, w_ada, b_ada):
    rows, d = c_pad.shape
    n_out = w_ada.shape[1]
    return pl.pallas_call(
        _mod_kernel,
        grid=(n_out // d,),
        in_specs=[
            pl.BlockSpec((rows, d), lambda j: (0, 0)),
            pl.BlockSpec((d, d), lambda j: (0, j)),
            pl.BlockSpec((1, d), lambda j: (0, j)),
        ],
        out_specs=pl.BlockSpec((rows, d), lambda j: (0, j)),
        out_shape=jax.ShapeDtypeStruct((rows, n_out), F32),
        compiler_params=pltpu.CompilerParams(
            dimension_semantics=("arbitrary",), vmem_limit_bytes=VMEM_LIMIT_BYTES),
        name="adaln_mod",
    )(c_pad, w_ada, b_ada)


_C_CONV = 0
_C_Q = 4 * CONV_WIDTH
_C_K = _C_Q + ATTN_WIDTH
_C_AZ = _C_K + ATTN_WIDTH
_C_QI = _C_AZ + ATTN_WIDTH
_C_KI = _C_QI + IDX_HEADS * IDX_DIM
_W1_COLS = _C_KI + 2 * IDX_DIM
_W2_ROWS = ATTN_WIDTH + 16


def _proj_kernel(x_ref, mod_ref, w1_ref, w2t_ref, cw_ref, cb_ref, lng_ref, lnb_ref,
                 gconv_ref, gmat_ref,
                 yc_ref, q_ref, k_ref, gz_ref, qi_ref, ki2_ref, vt_ref, wit_ref,
                 u_scr):
    i = pl.program_id(1)
    x = x_ref[...]
    ms = jnp.mean(x * x, axis=-1, keepdims=True)
    xn = x * lax.rsqrt(ms + NORM_EPS)
    shift = mod_ref[0:1, :]
    scale = mod_ref[1:2, :]
    h = (xn * (1.0 + scale) + shift).astype(BF16)

    def proj(c0, width):
        return jnp.dot(h, w1_ref[:, c0:c0 + width], preferred_element_type=F32)

    pc = proj(_C_CONV, 4 * CONV_WIDTH)
    b_gate = pc[:, 0:CONV_WIDTH]
    c_gate = pc[:, CONV_WIDTH:2 * CONV_WIDTH]
    x_in = pc[:, 2 * CONV_WIDTH:3 * CONV_WIDTH]
    z_conv = pc[:, 3 * CONV_WIDTH:4 * CONV_WIDTH]
    u = c_gate * x_in

    @pl.when(i == 0)
    def _():
        u_scr[0:SUBLANES, :] = jnp.zeros((SUBLANES, CONV_WIDTH), F32)

    u_scr[SUBLANES:SUBLANES + TM, :] = u
    u_m1 = u_scr[SUBLANES - 1:SUBLANES - 1 + TM, :]
    u_m2 = u_scr[SUBLANES - 2:SUBLANES - 2 + TM, :]
    y = cb_ref[...] + cw_ref[0:1, :] * u_m2
    y = y + cw_ref[1:2, :] * u_m1
    y = y + cw_ref[2:3, :] * u
    conv_out = b_gate * y
    u_scr[0:SUBLANES, :] = u_scr[TM:TM + SUBLANES, :]

    sq = conv_out * conv_out
    sq_hi = sq.astype(BF16)
    sq_lo = (sq - sq_hi.astype(F32)).astype(BF16)
    gms = (jnp.dot(sq_hi, gmat_ref[...], preferred_element_type=F32)
           + jnp.dot(sq_lo, gmat_ref[...], preferred_element_type=F32))
    yc = conv_out * lax.rsqrt(gms + NORM_EPS) * gconv_ref[...] * jax.nn.silu(z_conv)
    yc_ref[...] = yc.astype(BF16)

    q_ref[...] = (proj(_C_Q, ATTN_WIDTH) * (HEAD_DIM ** -0.5)).astype(BF16)
    k_ref[...] = proj(_C_K, ATTN_WIDTH).astype(BF16)
    gz_ref[...] = jax.nn.silu(proj(_C_AZ, ATTN_WIDTH))
    qi_ref[...] = proj(_C_QI, IDX_HEADS * IDX_DIM).astype(BF16)

    pki = proj(_C_KI, 2 * IDX_DIM)
    mu = jnp.mean(pki, axis=-1, keepdims=True)
    var = jnp.mean(jnp.square(pki - mu), axis=-1, keepdims=True)
    ki = (pki - mu) * lax.rsqrt(var + NORM_EPS) * lng_ref[...] + lnb_ref[...]
    ki2_ref[...] = ki.astype(BF16)

    pt = lax.dot_general(w2t_ref[...], h, NT_DIMS, preferred_element_type=F32)
    ones_rows = jnp.where(lax.broadcasted_iota(I32, (PACK16, TM), 0) == 0, 1.0, 0.0).astype(BF16)
    for hd in range(ATTN_HEADS):
        vt_ref[hd * VT_ROWS:hd * VT_ROWS + HEAD_DIM, :] = pt[
            hd * HEAD_DIM:(hd + 1) * HEAD_DIM, :].astype(BF16)
        vt_ref[hd * VT_ROWS + HEAD_DIM:(hd + 1) * VT_ROWS, :] = ones_rows
    idx_scale = (IDX_DIM ** -0.5) * (IDX_HEADS ** -0.5)
    wit_ref[...] = pt[ATTN_WIDTH:ATTN_WIDTH + IDX_HEADS, :] * idx_scale


def _projection(x, mod3, w1, w2t, conv_w, conv_b, lng2, lnb2, gconv, gmat):
    B, S, D = x.shape
    n_t = S // TM
    const2 = lambda b, i: (0, 0)
    out_shapes = (
        jax.ShapeDtypeStruct((B, S, CONV_WIDTH), BF16),
        jax.ShapeDtypeStruct((B, S, ATTN_WIDTH), BF16),
        jax.ShapeDtypeStruct((B, S, ATTN_WIDTH), BF16),
        jax.ShapeDtypeStruct((B, S, ATTN_WIDTH), F32),
        jax.ShapeDtypeStruct((B, S, IDX_HEADS * IDX_DIM), BF16),
        jax.ShapeDtypeStruct((B, S, 2 * IDX_DIM), BF16),
        jax.ShapeDtypeStruct((B, n_t, ATTN_HEADS * VT_ROWS, TM), BF16),
        jax.ShapeDtypeStruct((B, IDX_HEADS, S), F32),
    )
    tok = lambda w: pl.BlockSpec((None, TM, w), lambda b, i: (b, i, 0))
    return pl.pallas_call(
        _proj_kernel,
        grid=(B, n_t),
        in_specs=[
            pl.BlockSpec((None, TM, D), lambda b, i: (b, i, 0)),
            pl.BlockSpec((None, 3, D), lambda b, i: (b, 0, 0)),
            pl.BlockSpec((D, _W1_COLS), const2),
            pl.BlockSpec((_W2_ROWS, D), const2),
            pl.BlockSpec((CONV_K, CONV_WIDTH), const2),
            pl.BlockSpec((1, CONV_WIDTH), const2),
            pl.BlockSpec((1, 2 * IDX_DIM), const2),
            pl.BlockSpec((1, 2 * IDX_DIM), const2),
            pl.BlockSpec((1, CONV_WIDTH), const2),
            pl.BlockSpec((CONV_WIDTH, CONV_WIDTH), const2),
        ],
        out_specs=(
            tok(CONV_WIDTH), tok(ATTN_WIDTH), tok(ATTN_WIDTH), tok(ATTN_WIDTH),
            tok(IDX_HEADS * IDX_DIM), tok(2 * IDX_DIM),
            pl.BlockSpec((None, None, ATTN_HEADS * VT_ROWS, TM), lambda b, i: (b, i, 0, 0)),
            pl.BlockSpec((None, IDX_HEADS, TM), lambda b, i: (b, 0, i)),
        ),
        out_shape=out_shapes,
        scratch_shapes=[pltpu.VMEM((TM + 2 * SUBLANES, CONV_WIDTH), F32)],
        compiler_params=pltpu.CompilerParams(
            dimension_semantics=("arbitrary", "arbitrary"),
            vmem_limit_bytes=VMEM_LIMIT_BYTES),
        name="proj_conv",
    )(x, mod3, w1, w2t, conv_w, conv_b, lng2, lnb2, gconv, gmat)


def _tree(op, xs):
    xs = list(xs)
    while len(xs) > 1:
        xs = [op(xs[i], xs[i + 1]) if i + 1 < len(xs) else xs[i]
              for i in range(0, len(xs), 2)]
    return xs[0]


def _fold_rows(op, x, rows):
    return _tree(op, [x[r:r + rows] for r in range(0, x.shape[0], rows)])


def _attn_kernel(topk, seq_len,
                 x_ref, yc_ref, gz_ref, q_ref, qi_ref, wit_ref, qin_ref, witn_ref,
                 k_ref, vt_ref, ki2_ref, gate_ref, wout_ref, gattn_ref, gfin_ref,
                 o_ref,
                 sc_scr, sb_scr, mb_scr, lga_scr, lgb_scr, acc_scr, m_scr, l_scr, pos_scr,
                 qiw_scr, qw_scr):
    qb = pl.program_id(1)
    t0 = qb * TQ
    n_chunks = lax.shift_right_logical(t0 + (TQ - 1), LOG2_CK) + 1
    slopes = _alibi_slopes(ATTN_HEADS)

    def chunk_loop(body, init):
        return lax.fori_loop(
            0, n_chunks, lambda c, carry: body(c, pl.multiple_of(c * CK, CK), carry), init)

    lane_k = lax.broadcasted_iota(I32, (CK, LANES), 1)
    row_k = lax.broadcasted_iota(I32, (CK, LANES), 0)

    @pl.when(qb == 0)
    def _():
        def body(c, carry):
            base = pl.multiple_of(c * CK, CK)
            s = base + row_k
            val = jnp.where(lane_k == 0, lax.shift_right_logical(s, 6),
                            jnp.where(lane_k == 1, s & 63,
                                      jnp.where(lane_k <= 3, 1, 0)))
            pos_scr[pl.ds(base, CK), :] = val.astype(F32).astype(BF16)
            return carry
        lax.fori_loop(0, seq_len // CK, body, 0)

    lane = lax.broadcasted_iota(I32, (TQ, LANES), 1)
    t_row = t0 + lax.broadcasted_iota(I32, (TQ, LANES), 0)
    t_hi = lax.shift_right_logical(t_row, 6).astype(F32)
    t_lo = (t_row & 63).astype(F32)
    lo_half = lane < HEAD_DIM
    for j in range(N_PAIRS):
        pair = q_ref[:, j * LANES:(j + 1) * LANES]
        zero = jnp.zeros_like(pair)
        for e in range(2):
            sl = slopes[2 * j + e]
            rows = slice(e * TQ, (e + 1) * TQ)
            keep = lo_half if e == 0 else jnp.logical_not(lo_half)
            qw_scr[j, rows, 0:LANES] = jnp.where(keep, pair, zero)
            pcol = jnp.where(lane == 0, 64.0 * sl,
                             jnp.where(lane == 1, sl,
                                       jnp.where(lane == 2, (-64.0 * sl) * t_hi,
                                                 jnp.where(lane == 3, (-sl) * t_lo, 0.0))))
            qw_scr[j, rows, LANES:2 * LANES] = pcol.astype(BF16)

    row = lax.broadcasted_iota(I32, (CK, TQ), 0)
    q_lane = lax.broadcasted_iota(I32, (CK, TQ), 1)
    t_idx = t0 + q_lane

    def index_keys(qi_src, wit_src, blk):
        blk_t0 = blk * TQ
        for j in range(N_PAIRS):
            pair = qi_src[:, j * LANES:(j + 1) * LANES]
            zero = jnp.zeros_like(pair)
            qiw_scr[j, 0:TQ, :] = jnp.where(lo_half, pair, zero)
            qiw_scr[j, TQ:2 * TQ, :] = jnp.where(lo_half, zero, pair)

        def score_body(c, carry):
            base = pl.multiple_of(c * CK, CK)
            kic = ki2_ref[pl.ds(base, CK), :]
            score = None
            for j in range(N_PAIRS):
                d = lax.dot_general(kic, qiw_scr[j], NT_DIMS, preferred_element_type=F32)
                r = jnp.maximum(d, 0.0)
                term = (r[:, 0:TQ] * wit_src[2 * j:2 * j + 1, :]
                        + r[:, TQ:2 * TQ] * wit_src[2 * j + 1:2 * j + 2, :])
                score = term if score is None else score + term
            score = jnp.where(base + row <= blk_t0 + q_lane, score, -jnp.inf)
            sc_scr[pl.ds(base, CK), :] = score
            sb_scr[pl.ds(base, CK), :] = score.astype(BF16)
            return carry

        blk_chunks = lax.shift_right_logical(blk_t0 + (TQ - 1), LOG2_CK) + 1
        lax.fori_loop(0, blk_chunks, score_body, 0)

    @pl.when(qb == 0)
    def _():
        index_keys(qi_ref, wit_ref, qb)

    n_acc = 4
    one16 = jnp.ones((PACK16, TQ), I16)
    zero16 = jnp.zeros((PACK16, TQ), I16)
    zero_row = jnp.zeros((1, TQ), I32)
    n_noncausal = (seq_len - 1) - (t0 + lax.broadcasted_iota(I32, (1, TQ), 1))

    def key_to_f32(key):
        return pltpu.bitcast(jnp.where(key >= 0, key, key ^ 0x7FFFFFFF), F32)

    def count_bf16(cand):
        cand16 = jnp.broadcast_to(cand, (PACK16, TQ)).astype(BF16)

        def body(c, base, accs):
            accs = list(accs)
            blk = sb_scr[pl.ds(base, CK), :]
            for r in range(CK // PACK16):
                hit = blk[r * PACK16:(r + 1) * PACK16] >= cand16
                accs[r % n_acc] = accs[r % n_acc] + jnp.where(hit, one16, zero16)
            return tuple(accs)
        accs = chunk_loop(body, (zero16,) * n_acc)
        tot = ((accs[0] + accs[1]) + (accs[2] + accs[3])).astype(I32)
        return jnp.sum(tot, axis=0, keepdims=True)

    def count_f32(hit_fn):
        zero8 = jnp.zeros((SUBLANES, TQ), I32)

        def body(c, base, accs):
            accs = list(accs)
            blk = sc_scr[pl.ds(base, CK), :]
            for r in range(CK // SUBLANES):
                hit = hit_fn(blk[r * SUBLANES:(r + 1) * SUBLANES], base + r * SUBLANES)
                accs[r % n_acc] = accs[r % n_acc] + jnp.where(hit, 1, 0)
            return tuple(accs)
        accs = chunk_loop(body, (zero8,) * n_acc)
        return jnp.sum((accs[0] + accs[1]) + (accs[2] + accs[3]), axis=0, keepdims=True)

    def bf16_key(u16):
        k16 = u16 + I16_MIN
        return lax.shift_left(k16, 16) + jnp.where(k16 < 0, 0xFFFF, 0)

    def coarse_body(ib, u_thr):
        cand_u = u_thr | lax.shift_left(jnp.int32(1), 15 - ib)
        cand = key_to_f32(bf16_key(cand_u))
        total = count_bf16(cand) + jnp.where(cand <= NEG_INF_BF16, n_noncausal, 0)
        return jnp.where(total >= topk, cand_u, u_thr)

    u_coarse = lax.fori_loop(0, 16, coarse_body, zero_row)
    base_key = bf16_key(u_coarse) - 65536

    def fine_body(ib, off):
        cand_off = off | lax.shift_left(jnp.int32(1), 16 - ib)
        cand = key_to_f32(base_key + cand_off)
        total = (count_f32(lambda sv, _: sv >= cand)
                 + jnp.where(cand <= NEG_INF, n_noncausal, 0))
        return jnp.where(total >= topk, cand_off, off)

    thr = key_to_f32(base_key + lax.fori_loop(0, 17, fine_body, zero_row))

    def mask_body(c, base, acc):
        hit = sc_scr[pl.ds(base, CK), :] >= thr
        mb_scr[pl.ds(base, CK), :] = jnp.where(hit, 0.0, NEG_INF)
        return acc + _fold_rows(jnp.add, jnp.where(hit, 1, 0), SUBLANES)

    cnt_thr = jnp.sum(chunk_loop(mask_body, jnp.zeros((SUBLANES, TQ), I32)),
                      axis=0, keepdims=True)

    def mask_ties():
        sub = lax.broadcasted_iota(I32, (SUBLANES, TQ), 0)
        need = topk - count_f32(lambda sv, _: sv > thr)
        n_bits = (seq_len - 1).bit_length()

        def idx_body(ib, j_thr):
            cand = j_thr | lax.shift_left(jnp.int32(1), (n_bits - 1) - ib)
            cnt = count_f32(lambda sv, r0: jnp.logical_and(sv == thr, r0 + sub < cand))
            return jnp.where(cnt < need, cand, j_thr)

        j_thr = lax.fori_loop(0, n_bits, idx_body, zero_row)

        def body(c, base, carry):
            kv = sc_scr[pl.ds(base, CK), :]
            tie_ok = jnp.where(base + row <= j_thr, 0.0, NEG_INF)
            mb_scr[pl.ds(base, CK), :] = jnp.where(
                kv > thr, 0.0, jnp.where(kv == thr, tie_ok, NEG_INF))
            return carry
        chunk_loop(body, 0)

    lax.cond(jnp.max(cnt_thr) > topk, mask_ties, lambda: None)

    def logits_chunk(c, dst):
        base = pl.multiple_of(c * CK, CK)
        mb = mb_scr[pl.ds(base, CK), :]
        pos = pos_scr[pl.ds(base, CK), :]
        for j in range(N_PAIRS):
            lhs = jnp.concatenate([k_ref[pl.ds(base, CK), j * LANES:(j + 1) * LANES], pos],
                                  axis=1)
            lg = lax.dot_general(lhs, qw_scr[j], NT_DIMS, preferred_element_type=F32)
            for e in range(2):
                dst[2 * j + e] = lg[:, e * TQ:(e + 1) * TQ] + mb

    def softmax_chunk(c, src):
        for hd in range(ATTN_HEADS):
            m_old = m_scr[hd:hd + 1, :]
            slab_max = _tree(jnp.maximum, [
                _fold_rows(jnp.maximum, src[hd, r:r + KT // 4, :], SUBLANES)
                for r in range(0, CK, KT // 4)])
            m_new = jnp.maximum(m_old, jnp.max(slab_max, axis=0, keepdims=True))
            alpha = jnp.exp(m_old - m_new)
            pv = None
            for kt in range(0, CK, KT):
                p = jnp.exp(src[hd, kt:kt + KT, :] - m_new)
                d = jnp.dot(vt_ref[c, hd * VT_ROWS:(hd + 1) * VT_ROWS, kt:kt + KT],
                            p.astype(BF16), preferred_element_type=F32)
                pv = d if pv is None else pv + d
            l_scr[hd:hd + 1, :] = alpha * l_scr[hd:hd + 1, :] + pv[HEAD_DIM:HEAD_DIM + 1, :]
            m_scr[hd:hd + 1, :] = m_new
            rows = slice(hd * HEAD_DIM, (hd + 1) * HEAD_DIM)
            acc_scr[rows, :] = alpha * acc_scr[rows, :] + pv[0:HEAD_DIM, :]

    m_scr[...] = jnp.full((ATTN_HEADS, TQ), NEG_INF, F32)
    l_scr[...] = jnp.zeros((ATTN_HEADS, TQ), F32)
    acc_scr[...] = jnp.zeros((ATTN_WIDTH, TQ), F32)
    logits_chunk(0, lga_scr)
    last = n_chunks - 1

    def attn_body(t, carry):
        c = 2 * t
        logits_chunk(jnp.minimum(c + 1, last), lgb_scr)
        softmax_chunk(c, lga_scr)

        @pl.when(c < last)
        def _():
            logits_chunk(jnp.minimum(c + 2, last), lga_scr)
            softmax_chunk(c + 1, lgb_scr)
        return carry

    lax.fori_loop(0, lax.shift_right_logical(n_chunks + 1, 1), attn_body, 0)

    normed = []
    for hd in range(ATTN_HEADS):
        rows = slice(hd * HEAD_DIM, (hd + 1) * HEAD_DIM)
        o = acc_scr[rows, :] / l_scr[hd:hd + 1, :]
        ms = jnp.mean(o * o, axis=0, keepdims=True)
        normed.append(o * lax.rsqrt(ms + NORM_EPS))
    attn_t = jnp.concatenate(normed, axis=0)
    ya = attn_t.T * gattn_ref[...] * gz_ref[...]
    upd = (jnp.dot(yc_ref[...], wout_ref[0:CONV_WIDTH, :], preferred_element_type=F32)
           + jnp.dot(ya.astype(BF16), wout_ref[CONV_WIDTH:, :],
                     preferred_element_type=F32))
    xr = x_ref[...] + gate_ref[...] * upd
    ms = jnp.mean(xr * xr, axis=-1, keepdims=True)
    o_ref[...] = xr * lax.rsqrt(ms + NORM_EPS) * gfin_ref[...]

    @pl.when(qb + 1 < pl.num_programs(1))
    def _():
        index_keys(qin_ref, witn_ref, qb + 1)


def _attention(x, yc, gz, q, qi, wit, k, vt, ki2, gate, wout, gattn, gfin):
    B, S, D = x.shape
    topk = min(TOPK_MAX, S // 4)
    n_q = S // TQ
    tok = lambda w: pl.BlockSpec((None, TQ, w), lambda b, i: (b, i, 0))
    const2 = lambda b, i: (0, 0)
    once = dict(pipeline_mode=pl.Buffered(1))
    nxt_tok = lambda b, i: (b, jnp.minimum(i + 1, n_q - 1), 0)
    nxt_row = lambda b, i: (b, 0, jnp.minimum(i + 1, n_q - 1))
    return pl.pallas_call(
        functools.partial(_attn_kernel, topk, S),
        grid=(B, n_q),
        in_specs=[
            tok(D),
            tok(CONV_WIDTH),
            tok(ATTN_WIDTH),
            tok(ATTN_WIDTH),
            tok(IDX_HEADS * IDX_DIM),
            pl.BlockSpec((None, IDX_HEADS, TQ), lambda b, i: (b, 0, i)),
            pl.BlockSpec((None, TQ, IDX_HEADS * IDX_DIM), nxt_tok),
            pl.BlockSpec((None, IDX_HEADS, TQ), nxt_row),
            pl.BlockSpec((None, S, ATTN_WIDTH), lambda b, i: (b, 0, 0), **once),
            pl.BlockSpec((None, S // CK, ATTN_HEADS * VT_ROWS, CK), lambda b, i: (b, 0, 0, 0),
                         **once),
            pl.BlockSpec((None, S, 2 * IDX_DIM), lambda b, i: (b, 0, 0), **once),
            pl.BlockSpec((None, 1, D), lambda b, i: (b, 0, 0)),
            pl.BlockSpec((D, D), const2, **once),
            pl.BlockSpec((1, ATTN_WIDTH), const2),
            pl.BlockSpec((1, D), const2),
        ],
        out_specs=tok(D),
        out_shape=jax.ShapeDtypeStruct((B, S, D), F32),
        scratch_shapes=[
            pltpu.VMEM((S, TQ), F32),
            pltpu.VMEM((S, TQ), BF16),
            pltpu.VMEM((S, TQ), F32),
            pltpu.VMEM((ATTN_HEADS, CK, TQ), F32),
            pltpu.VMEM((ATTN_HEADS, CK, TQ), F32),
            pltpu.VMEM((ATTN_WIDTH, TQ), F32),
            pltpu.VMEM((ATTN_HEADS, TQ), F32),
            pltpu.VMEM((ATTN_HEADS, TQ), F32),
            pltpu.VMEM((S, LANES), BF16),
            pltpu.VMEM((N_PAIRS, 2 * TQ, LANES), BF16),
            pltpu.VMEM((N_PAIRS, 2 * TQ, 2 * LANES), BF16),
        ],
        compiler_params=pltpu.CompilerParams(
            dimension_semantics=("arbitrary", "arbitrary"),
            vmem_limit_bytes=VMEM_LIMIT_BYTES),
        name="dsa_attn_out",
    )(x, yc, gz, q, qi, wit, qi, wit, k, vt, ki2, gate, wout, gattn, gfin)


def kernel(x, c, w_ada, b_ada, w_in, conv_w, conv_b, idx_k_gain, idx_k_bias,
           mix_norm_gain, w_out, final_gain):
    B, S, D = x.shape
    assert w_ada.shape[0] == 1 and B <= SUBLANES
    assert D == D_MODEL and S % TM == 0 and TM == CK and CK % TQ == 0 and CK == 2 ** LOG2_CK
    c_pad = jnp.zeros((SUBLANES, D), F32).at[:B].set(c)
    grp = jnp.arange(CONV_WIDTH) // GROUP_DIM
    gmat = jnp.where(grp[:, None] == grp[None, :], 1.0 / GROUP_DIM, 0.0).astype(BF16)
    col = lambda w, a, n: w[:, a:a + n]
    o_q = 4 * CONV_WIDTH
    o_k, o_v, o_az = o_q + ATTN_WIDTH, o_q + 2 * ATTN_WIDTH, o_q + 3 * ATTN_WIDTH
    o_qi = o_q + 4 * ATTN_WIDTH
    o_ki = o_qi + IDX_HEADS * IDX_DIM
    o_wi = o_ki + IDX_DIM
    wl = w_in[0]
    w1 = jnp.concatenate(
        [col(wl, 0, 4 * CONV_WIDTH), col(wl, o_q, ATTN_WIDTH), col(wl, o_k, ATTN_WIDTH),
         col(wl, o_az, ATTN_WIDTH), col(wl, o_qi, IDX_HEADS * IDX_DIM),
         col(wl, o_ki, IDX_DIM), col(wl, o_ki, IDX_DIM)], axis=1).astype(BF16)
    w2t = jnp.concatenate(
        [col(wl, o_v, ATTN_WIDTH).T, col(wl, o_wi, IDX_HEADS).T,
         jnp.zeros((_W2_ROWS - ATTN_WIDTH - IDX_HEADS, D), F32)], axis=0).astype(BF16)
    mod = _modulation(c_pad, w_ada[0], b_ada[0][None, :])[:B]
    mod3 = mod.reshape(B, 3, D)
    dup = lambda v: jnp.concatenate([v, v])[None, :]
    yc, q, k, gz, qi, ki2, vt, wit = _projection(
        x, mod3, w1, w2t, conv_w[0], conv_b[0][None, :],
        dup(idx_k_gain[0]), dup(idx_k_bias[0]),
        mix_norm_gain[0][None, :CONV_WIDTH], gmat)
    return _attention(x, yc, gz, q, qi, wit, k, vt, ki2, mod3[:, 2:3, :],
                      w_out[0].astype(BF16), mix_norm_gain[0][None, CONV_WIDTH:],
                      final_gain[None, :])
```

```python
import functools

import numpy as np
import jax
import jax.numpy as jnp
from jax import lax
from jax.experimental import pallas as pl
from jax.experimental.pallas import tpu as pltpu

F32 = jnp.float32
BF16 = jnp.bfloat16
I32 = jnp.int32
I16 = jnp.int16

D_MODEL = 1024
CONV_WIDTH = 512
CONV_K = 3
ATTN_HEADS = 8
HEAD_DIM = 64
ATTN_WIDTH = ATTN_HEADS * HEAD_DIM
IDX_HEADS = 8
IDX_DIM = 64
TOPK_MAX = 256
GROUP_DIM = 64
NORM_EPS = 1e-6
NEG_INF = -1e30

LANES = 128
SUBLANES = 8
PACK16 = 2 * SUBLANES
VT_ROWS = HEAD_DIM + PACK16
VMEM_LIMIT_BYTES = 56 * 1024 * 1024

TM = 512
TQ = 256
CK = 512
LOG2_CK = 9
KT = 256
N_PAIRS = ATTN_HEADS // 2

I16_MIN = -(2 ** 15)
NEG_INF_BF16 = float(np.asarray(NEG_INF, dtype=BF16).astype(np.float32))

NT_DIMS = (((1,), (1,)), ((), ()))


def _alibi_slopes(n_heads):
    return [2.0 ** (-8.0 * (i + 1) / n_heads) for i in range(n_heads)]


def _mod_kernel(c_ref, w_ref, b_ref, o_ref):
    c_act = jax.nn.silu(c_ref[...])
    o_ref[...] = jnp.dot(c_act.astype(BF16), w_ref[...].astype(BF16),
                         preferred_element_type=F32) + b_ref[...]


def _modulation(c_pad, w_ada, b_ada):
    rows, d = c_pad.shape
    n_out = w_ada.shape[1]
    return pl.pallas_call(
        _mod_kernel,
        grid=(n_out // d,),
        in_specs=[
            pl.BlockSpec((rows, d), lambda j: (0, 0)),
            pl.BlockSpec((d, d), lambda j: (0, j)),
            pl.BlockSpec((1, d), lambda j: (0, j)),
        ],
        out_specs=pl.BlockSpec((rows, d), lambda j: (0, j)),
        out_shape=jax.ShapeDtypeStruct((rows, n_out), F32),
        compiler_params=pltpu.CompilerParams(
            dimension_semantics=("arbitrary",), vmem_limit_bytes=VMEM_LIMIT_BYTES),
        name="adaln_mod",
    )(c_pad, w_ada, b_ada)


_C_CONV = 0
_C_Q = 4 * CONV_WIDTH
_C_K = _C_Q + ATTN_WIDTH
_C_AZ = _C_K + ATTN_WIDTH
_C_QI = _C_AZ + ATTN_WIDTH
_C_KI = _C_QI + IDX_HEADS * IDX_DIM
_W1_COLS = _C_KI + 2 * IDX_DIM
_W2_ROWS = ATTN_WIDTH + 16


def _proj_kernel(x_ref, mod_ref, w1_ref, w2t_ref, cw_ref, cb_ref, lng_ref, lnb_ref,
                 gconv_ref, gmat_ref,
                 yc_ref, q_ref, k_ref, gz_ref, qi_ref, ki2_ref, vt_ref, wit_ref,
                 u_scr):
    i = pl.program_id(1)
    x = x_ref[...]
    ms = jnp.mean(x * x, axis=-1, keepdims=True)
    xn = x * lax.rsqrt(ms + NORM_EPS)
    shift = mod_ref[0:1, :]
    scale = mod_ref[1:2, :]
    h = (xn * (1.0 + scale) + shift).astype(BF16)

    def proj(c0, width):
        return jnp.dot(h, w1_ref[:, c0:c0 + width], preferred_element_type=F32)

    pc = proj(_C_CONV, 4 * CONV_WIDTH)
    b_gate = pc[:, 0:CONV_WIDTH]
    c_gate = pc[:, CONV_WIDTH:2 * CONV_WIDTH]
    x_in = pc[:, 2 * CONV_WIDTH:3 * CONV_WIDTH]
    z_conv = pc[:, 3 * CONV_WIDTH:4 * CONV_WIDTH]
    u = c_gate * x_in

    @pl.when(i == 0)
    def _():
        u_scr[0:SUBLANES, :] = jnp.zeros((SUBLANES, CONV_WIDTH), F32)

    u_scr[SUBLANES:SUBLANES + TM, :] = u
    u_m1 = u_scr[SUBLANES - 1:SUBLANES - 1 + TM, :]
    u_m2 = u_scr[SUBLANES - 2:SUBLANES - 2 + TM, :]
    y = cb_ref[...] + cw_ref[0:1, :] * u_m2
    y = y + cw_ref[1:2, :] * u_m1
    y = y + cw_ref[2:3, :] * u
    conv_out = b_gate * y
    u_scr[0:SUBLANES, :] = u_scr[TM:TM + SUBLANES, :]

    sq = conv_out * conv_out
    sq_hi = sq.astype(BF16)
    sq_lo = (sq - sq_hi.astype(F32)).astype(BF16)
    gms = (jnp.dot(sq_hi, gmat_ref[...], preferred_element_type=F32)
           + jnp.dot(sq_lo, gmat_ref[...], preferred_element_type=F32))
    yc = conv_out * lax.rsqrt(gms + NORM_EPS) * gconv_ref[...] * jax.nn.silu(z_conv)
    yc_ref[...] = yc.astype(BF16)

    q_ref[...] = (proj(_C_Q, ATTN_WIDTH) * (HEAD_DIM ** -0.5)).astype(BF16)
    k_ref[...] = proj(_C_K, ATTN_WIDTH).astype(BF16)
    gz_ref[...] = jax.nn.silu(proj(_C_AZ, ATTN_WIDTH))
    qi_ref[...] = proj(_C_QI, IDX_HEADS * IDX_DIM).astype(BF16)

    pki = proj(_C_KI, 2 * IDX_DIM)
    mu = jnp.mean(pki, axis=-1, keepdims=True)
    var = jnp.mean(jnp.square(pki - mu), axis=-1, keepdims=True)
    ki = (pki - mu) * lax.rsqrt(var + NORM_EPS) * lng_ref[...] + lnb_ref[...]
    ki2_ref[...] = ki.astype(BF16)

    pt = lax.dot_general(w2t_ref[...], h, NT_DIMS, preferred_element_type=F32)
    ones_rows = jnp.where(lax.broadcasted_iota(I32, (PACK16, TM), 0) == 0, 1.0, 0.0).astype(BF16)
    for hd in range(ATTN_HEADS):
        vt_ref[hd * VT_ROWS:hd * VT_ROWS + HEAD_DIM, :] = pt[
            hd * HEAD_DIM:(hd + 1) * HEAD_DIM, :].astype(BF16)
        vt_ref[hd * VT_ROWS + HEAD_DIM:(hd + 1) * VT_ROWS, :] = ones_rows
    idx_scale = (IDX_DIM ** -0.5) * (IDX_HEADS ** -0.5)
    wit_ref[...] = pt[ATTN_WIDTH:ATTN_WIDTH + IDX_HEADS, :] * idx_scale


def _projection(x, mod3, w1, w2t, conv_w, conv_b, lng2, lnb2, gconv, gmat):
    B, S, D = x.shape
    n_t = S // TM
    const2 = lambda b, i: (0, 0)
    out_shapes = (
        jax.ShapeDtypeStruct((B, S, CONV_WIDTH), BF16),
        jax.ShapeDtypeStruct((B, S, ATTN_WIDTH), BF16),
        jax.ShapeDtypeStruct((B, S, ATTN_WIDTH), BF16),
        jax.ShapeDtypeStruct((B, S, ATTN_WIDTH), F32),
        jax.ShapeDtypeStruct((B, S, IDX_HEADS * IDX_DIM), BF16),
        jax.ShapeDtypeStruct((B, S, 2 * IDX_DIM), BF16),
        jax.ShapeDtypeStruct((B, n_t, ATTN_HEADS * VT_ROWS, TM), BF16),
        jax.ShapeDtypeStruct((B, IDX_HEADS, S), F32),
    )
    tok = lambda w: pl.BlockSpec((None, TM, w), lambda b, i: (b, i, 0))
    return pl.pallas_call(
        _proj_kernel,
        grid=(B, n_t),
        in_specs=[
            pl.BlockSpec((None, TM, D), lambda b, i: (b, i, 0)),
            pl.BlockSpec((None, 3, D), lambda b, i: (b, 0, 0)),
            pl.BlockSpec((D, _W1_COLS), const2),
            pl.BlockSpec((_W2_ROWS, D), const2),
            pl.BlockSpec((CONV_K, CONV_WIDTH), const2),
            pl.BlockSpec((1, CONV_WIDTH), const2),
            pl.BlockSpec((1, 2 * IDX_DIM), const2),
            pl.BlockSpec((1, 2 * IDX_DIM), const2),
            pl.BlockSpec((1, CONV_WIDTH), const2),
            pl.BlockSpec((CONV_WIDTH, CONV_WIDTH), const2),
        ],
        out_specs=(
            tok(CONV_WIDTH), tok(ATTN_WIDTH), tok(ATTN_WIDTH), tok(ATTN_WIDTH),
            tok(IDX_HEADS * IDX_DIM), tok(2 * IDX_DIM),
            pl.BlockSpec((None, None, ATTN_HEADS * VT_ROWS, TM), lambda b, i: (b, i, 0, 0)),
            pl.BlockSpec((None, IDX_HEADS, TM), lambda b, i: (b, 0, i)),
        ),
        out_shape=out_shapes,
        scratch_shapes=[pltpu.VMEM((TM + 2 * SUBLANES, CONV_WIDTH), F32)],
        compiler_params=pltpu.CompilerParams(
            dimension_semantics=("arbitrary", "arbitrary"),
            vmem_limit_bytes=VMEM_LIMIT_BYTES),
        name="proj_conv",
    )(x, mod3, w1, w2t, conv_w, conv_b, lng2, lnb2, gconv, gmat)


def _tree(op, xs):
    xs = list(xs)
    while len(xs) > 1:
        xs = [op(xs[i], xs[i + 1]) if i + 1 < len(xs) else xs[i]
              for i in range(0, len(xs), 2)]
    return xs[0]


def _fold_rows(op, x, rows):
    return _tree(op, [x[r:r + rows] for r in range(0, x.shape[0], rows)])


def _attn_kernel(topk, seq_len,
                 x_ref, yc_ref, gz_ref, q_ref, qi_ref, wit_ref, qin_ref, witn_ref,
                 k_ref, vt_ref, ki2_ref, gate_ref, wout_ref, gattn_ref, gfin_ref,
                 o_ref,
                 sc_scr, sb_scr, accn_scr, mb_scr, lga_scr, lgb_scr, acc_scr, m_scr, l_scr, pos_scr,
                 qiw_scr, qw_scr):
    qb = pl.program_id(1)
    t0 = qb * TQ
    n_chunks = lax.shift_right_logical(t0 + (TQ - 1), LOG2_CK) + 1
    slopes = _alibi_slopes(ATTN_HEADS)

    def chunk_loop(body, init):
        return lax.fori_loop(
            0, n_chunks, lambda c, carry: body(c, pl.multiple_of(c * CK, CK), carry), init)

    lane_k = lax.broadcasted_iota(I32, (CK, LANES), 1)
    row_k = lax.broadcasted_iota(I32, (CK, LANES), 0)

    @pl.when(qb == 0)
    def _():
        def body(c, carry):
            base = pl.multiple_of(c * CK, CK)
            s = base + row_k
            val = jnp.where(lane_k == 0, lax.shift_right_logical(s, 6),
                            jnp.where(lane_k == 1, s & 63,
                                      jnp.where(lane_k <= 3, 1, 0)))
            pos_scr[pl.ds(base, CK), :] = val.astype(F32).astype(BF16)
            return carry
        lax.fori_loop(0, seq_len // CK, body, 0)

    lane = lax.broadcasted_iota(I32, (TQ, LANES), 1)
    t_row = t0 + lax.broadcasted_iota(I32, (TQ, LANES), 0)
    t_hi = lax.shift_right_logical(t_row, 6).astype(F32)
    t_lo = (t_row & 63).astype(F32)
    lo_half = lane < HEAD_DIM
    for j in range(N_PAIRS):
        pair = q_ref[:, j * LANES:(j + 1) * LANES]
        zero = jnp.zeros_like(pair)
        for e in range(2):
            sl = slopes[2 * j + e]
            rows = slice(e * TQ, (e + 1) * TQ)
            keep = lo_half if e == 0 else jnp.logical_not(lo_half)
            qw_scr[j, rows, 0:LANES] = jnp.where(keep, pair, zero)
            pcol = jnp.where(lane == 0, 64.0 * sl,
                             jnp.where(lane == 1, sl,
                                       jnp.where(lane == 2, (-64.0 * sl) * t_hi,
                                                 jnp.where(lane == 3, (-sl) * t_lo, 0.0))))
            qw_scr[j, rows, LANES:2 * LANES] = pcol.astype(BF16)

    row = lax.broadcasted_iota(I32, (CK, TQ), 0)
    q_lane = lax.broadcasted_iota(I32, (CK, TQ), 1)
    t_idx = t0 + q_lane

    def build_qiw(qi_src):
        for j in range(N_PAIRS):
            pair = qi_src[:, j * LANES:(j + 1) * LANES]
            zero = jnp.zeros_like(pair)
            qiw_scr[j, 0:TQ, :] = jnp.where(lo_half, pair, zero)
            qiw_scr[j, TQ:2 * TQ, :] = jnp.where(lo_half, zero, pair)

    def score_pair(c, j, wit_src, blk_t0):
        base = pl.multiple_of(c * CK, CK)
        d = lax.dot_general(ki2_ref[pl.ds(base, CK), :], qiw_scr[j], NT_DIMS,
                            preferred_element_type=F32)
        r = jnp.maximum(d, 0.0)
        term = (r[:, 0:TQ] * wit_src[2 * j:2 * j + 1, :]
                + r[:, TQ:2 * TQ] * wit_src[2 * j + 1:2 * j + 2, :])
        if j == 0:
            accn_scr[...] = term
        elif j < N_PAIRS - 1:
            accn_scr[...] += term
        else:
            score = jnp.where(base + row <= blk_t0 + q_lane, accn_scr[...] + term, -jnp.inf)
            sc_scr[pl.ds(base, CK), :] = score
            sb_scr[pl.ds(base, CK), :] = score.astype(BF16)

    def score_chunk(c, wit_src, blk_t0):
        for j in range(N_PAIRS):
            score_pair(c, j, wit_src, blk_t0)

    @pl.when(qb == 0)
    def _():
        build_qiw(qi_ref)

        def body(c, carry):
            score_chunk(c, wit_ref, t0)
            return carry
        lax.fori_loop(0, n_chunks, body, 0)

    nxt_t0 = jnp.minimum(qb + 1, pl.num_programs(1) - 1) * TQ
    nxt_chunks = lax.shift_right_logical(nxt_t0 + (TQ - 1), LOG2_CK) + 1
    build_qiw(qin_ref)

    n_acc = 4
    one16 = jnp.ones((PACK16, TQ), I16)
    zero16 = jnp.zeros((PACK16, TQ), I16)
    zero_row = jnp.zeros((1, TQ), I32)
    n_noncausal = (seq_len - 1) - (t0 + lax.broadcasted_iota(I32, (1, TQ), 1))

    def key_to_f32(key):
        return pltpu.bitcast(jnp.where(key >= 0, key, key ^ 0x7FFFFFFF), F32)

    def count_bf16(cand):
        cand16 = jnp.broadcast_to(cand, (PACK16, TQ)).astype(BF16)

        def body(c, base, accs):
            accs = list(accs)
            blk = sb_scr[pl.ds(base, CK), :]
            for r in range(CK // PACK16):
                hit = blk[r * PACK16:(r + 1) * PACK16] >= cand16
                accs[r % n_acc] = accs[r % n_acc] + jnp.where(hit, one16, zero16)
            return tuple(accs)
        accs = chunk_loop(body, (zero16,) * n_acc)
        tot = ((accs[0] + accs[1]) + (accs[2] + accs[3])).astype(I32)
        return jnp.sum(tot, axis=0, keepdims=True)

    def count_f32(hit_fn):
        zero8 = jnp.zeros((SUBLANES, TQ), I32)

        def body(c, base, accs):
            accs = list(accs)
            blk = sc_scr[pl.ds(base, CK), :]
            for r in range(CK // SUBLANES):
                hit = hit_fn(blk[r * SUBLANES:(r + 1) * SUBLANES], base + r * SUBLANES)
                accs[r % n_acc] = accs[r % n_acc] + jnp.where(hit, 1, 0)
            return tuple(accs)
        accs = chunk_loop(body, (zero8,) * n_acc)
        return jnp.sum((accs[0] + accs[1]) + (accs[2] + accs[3]), axis=0, keepdims=True)

    def bf16_key(u16):
        k16 = u16 + I16_MIN
        return lax.shift_left(k16, 16) + jnp.where(k16 < 0, 0xFFFF, 0)

    def coarse_body(ib, u_thr):
        cand_u = u_thr | lax.shift_left(jnp.int32(1), 15 - ib)
        cand = key_to_f32(bf16_key(cand_u))
        total = count_bf16(cand) + jnp.where(cand <= NEG_INF_BF16, n_noncausal, 0)
        return jnp.where(total >= topk, cand_u, u_thr)

    u_coarse = lax.fori_loop(0, 16, coarse_body, zero_row)
    base_key = bf16_key(u_coarse) - 65536

    def fine_body(ib, off):
        cand_off = off | lax.shift_left(jnp.int32(1), 16 - ib)
        cand = key_to_f32(base_key + cand_off)
        total = (count_f32(lambda sv, _: sv >= cand)
                 + jnp.where(cand <= NEG_INF, n_noncausal, 0))
        return jnp.where(total >= topk, cand_off, off)

    thr = key_to_f32(base_key + lax.fori_loop(0, 17, fine_body, zero_row))

    def mask_body(c, base, acc):
        hit = sc_scr[pl.ds(base, CK), :] >= thr
        mb_scr[pl.ds(base, CK), :] = jnp.where(hit, 0.0, NEG_INF)
        return acc + _fold_rows(jnp.add, jnp.where(hit, 1, 0), SUBLANES)

    cnt_thr = jnp.sum(chunk_loop(mask_body, jnp.zeros((SUBLANES, TQ), I32)),
                      axis=0, keepdims=True)

    def mask_ties():
        sub = lax.broadcasted_iota(I32, (SUBLANES, TQ), 0)
        need = topk - count_f32(lambda sv, _: sv > thr)
        n_bits = (seq_len - 1).bit_length()

        def idx_body(ib, j_thr):
            cand = j_thr | lax.shift_left(jnp.int32(1), (n_bits - 1) - ib)
            cnt = count_f32(lambda sv, r0: jnp.logical_and(sv == thr, r0 + sub < cand))
            return jnp.where(cnt < need, cand, j_thr)

        j_thr = lax.fori_loop(0, n_bits, idx_body, zero_row)

        def body(c, base, carry):
            kv = sc_scr[pl.ds(base, CK), :]
            tie_ok = jnp.where(base + row <= j_thr, 0.0, NEG_INF)
            mb_scr[pl.ds(base, CK), :] = jnp.where(
                kv > thr, 0.0, jnp.where(kv == thr, tie_ok, NEG_INF))
            return carry
        chunk_loop(body, 0)

    lax.cond(jnp.max(cnt_thr) > topk, mask_ties, lambda: None)

    def logits_pair(c, j, dst):
        base = pl.multiple_of(c * CK, CK)
        mb = mb_scr[pl.ds(base, CK), :]
        lhs = jnp.concatenate([k_ref[pl.ds(base, CK), j * LANES:(j + 1) * LANES],
                               pos_scr[pl.ds(base, CK), :]], axis=1)
        lg = lax.dot_general(lhs, qw_scr[j], NT_DIMS, preferred_element_type=F32)
        for e in range(2):
            dst[2 * j + e] = lg[:, e * TQ:(e + 1) * TQ] + mb

    def logits_chunk(c, dst):
        for j in range(N_PAIRS):
            logits_pair(c, j, dst)

    def softmax_heads(c, src, heads):
        for hd in heads:
            m_old = m_scr[hd:hd + 1, :]
            slab_max = _tree(jnp.maximum, [
                _fold_rows(jnp.maximum, src[hd, r:r + KT // 4, :], SUBLANES)
                for r in range(0, CK, KT // 4)])
            m_new = jnp.maximum(m_old, jnp.max(slab_max, axis=0, keepdims=True))
            alpha = jnp.exp(m_old - m_new)
            pv = None
            for kt in range(0, CK, KT):
                p = jnp.exp(src[hd, kt:kt + KT, :] - m_new)
                d = jnp.dot(vt_ref[c, hd * VT_ROWS:(hd + 1) * VT_ROWS, kt:kt + KT],
                            p.astype(BF16), preferred_element_type=F32)
                pv = d if pv is None else pv + d
            l_scr[hd:hd + 1, :] = alpha * l_scr[hd:hd + 1, :] + pv[HEAD_DIM:HEAD_DIM + 1, :]
            m_scr[hd:hd + 1, :] = m_new
            rows = slice(hd * HEAD_DIM, (hd + 1) * HEAD_DIM)
            acc_scr[rows, :] = alpha * acc_scr[rows, :] + pv[0:HEAD_DIM, :]

    m_scr[...] = jnp.full((ATTN_HEADS, TQ), NEG_INF, F32)
    l_scr[...] = jnp.zeros((ATTN_HEADS, TQ), F32)
    acc_scr[...] = jnp.zeros((ATTN_WIDTH, TQ), F32)
    logits_chunk(0, lga_scr)
    last = n_chunks - 1

    def attn_half(c, src, dst):
        c_next = jnp.minimum(c + 1, last)
        for j in range(N_PAIRS):
            logits_pair(c_next, j, dst)
            score_pair(c, j, witn_ref, nxt_t0)
            softmax_heads(c, src, (2 * j, 2 * j + 1))

    def attn_body(t, carry):
        c = 2 * t
        attn_half(c, lga_scr, lgb_scr)

        @pl.when(c < last)
        def _():
            attn_half(c + 1, lgb_scr, lga_scr)
        return carry

    lax.fori_loop(0, lax.shift_right_logical(n_chunks + 1, 1), attn_body, 0)

    @pl.when(nxt_chunks > n_chunks)
    def _():
        score_chunk(n_chunks, witn_ref, nxt_t0)

    normed = []
    for hd in range(ATTN_HEADS):
        rows = slice(hd * HEAD_DIM, (hd + 1) * HEAD_DIM)
        o = acc_scr[rows, :] / l_scr[hd:hd + 1, :]
        ms = jnp.mean(o * o, axis=0, keepdims=True)
        normed.append(o * lax.rsqrt(ms + NORM_EPS))
    attn_t = jnp.concatenate(normed, axis=0)
    ya = attn_t.T * gattn_ref[...] * gz_ref[...]
    upd = (jnp.dot(yc_ref[...], wout_ref[0:CONV_WIDTH, :], preferred_element_type=F32)
           + jnp.dot(ya.astype(BF16), wout_ref[CONV_WIDTH:, :],
                     preferred_element_type=F32))
    xr = x_ref[...] + gate_ref[...] * upd
    ms = jnp.mean(xr * xr, axis=-1, keepdims=True)
    o_ref[...] = xr * lax.rsqrt(ms + NORM_EPS) * gfin_ref[...]


def _attention(x, yc, gz, q, qi, wit, k, vt, ki2, gate, wout, gattn, gfin):
    B, S, D = x.shape
    topk = min(TOPK_MAX, S // 4)
    n_q = S // TQ
    tok = lambda w: pl.BlockSpec((None, TQ, w), lambda b, i: (b, i, 0))
    const2 = lambda b, i: (0, 0)
    once = dict(pipeline_mode=pl.Buffered(1))
    nxt_tok = lambda b, i: (b, jnp.minimum(i + 1, n_q - 1), 0)
    nxt_row = lambda b, i: (b, 0, jnp.minimum(i + 1, n_q - 1))
    return pl.pallas_call(
        functools.partial(_attn_kernel, topk, S),
        grid=(B, n_q),
        in_specs=[
            tok(D),
            tok(CONV_WIDTH),
            tok(ATTN_WIDTH),
            tok(ATTN_WIDTH),
            tok(IDX_HEADS * IDX_DIM),
            pl.BlockSpec((None, IDX_HEADS, TQ), lambda b, i: (b, 0, i)),
            pl.BlockSpec((None, TQ, IDX_HEADS * IDX_DIM), nxt_tok),
            pl.BlockSpec((None, IDX_HEADS, TQ), nxt_row),
            pl.BlockSpec((None, S, ATTN_WIDTH), lambda b, i: (b, 0, 0), **once),
            pl.BlockSpec((None, S // CK, ATTN_HEADS * VT_ROWS, CK), lambda b, i: (b, 0, 0, 0),
                         **once),
            pl.BlockSpec((None, S, 2 * IDX_DIM), lambda b, i: (b, 0, 0), **once),
            pl.BlockSpec((None, 1, D), lambda b, i: (b, 0, 0)),
            pl.BlockSpec((D, D), const2, **once),
            pl.BlockSpec((1, ATTN_WIDTH), const2),
            pl.BlockSpec((1, D), const2),
        ],
        out_specs=tok(D),
        out_shape=jax.ShapeDtypeStruct((B, S, D), F32),
        scratch_shapes=[
            pltpu.VMEM((S, TQ), F32),
            pltpu.VMEM((S, TQ), BF16),
            pltpu.VMEM((CK, TQ), F32),
            pltpu.VMEM((S, TQ), F32),
            pltpu.VMEM((ATTN_HEADS, CK, TQ), F32),
            pltpu.VMEM((ATTN_HEADS, CK, TQ), F32),
            pltpu.VMEM((ATTN_WIDTH, TQ), F32),
            pltpu.VMEM((ATTN_HEADS, TQ), F32),
            pltpu.VMEM((ATTN_HEADS, TQ), F32),
            pltpu.VMEM((S, LANES), BF16),
            pltpu.VMEM((N_PAIRS, 2 * TQ, LANES), BF16),
            pltpu.VMEM((N_PAIRS, 2 * TQ, 2 * LANES), BF16),
        ],
        compiler_params=pltpu.CompilerParams(
            dimension_semantics=("arbitrary", "arbitrary"),
            vmem_limit_bytes=VMEM_LIMIT_BYTES),
        name="dsa_attn_out",
    )(x, yc, gz, q, qi, wit, qi, wit, k, vt, ki2, gate, wout, gattn, gfin)


def kernel(x, c, w_ada, b_ada, w_in, conv_w, conv_b, idx_k_gain, idx_k_bias,
           mix_norm_gain, w_out, final_gain):
    B, S, D = x.shape
    assert w_ada.shape[0] == 1 and B <= SUBLANES
    assert D == D_MODEL and S % TM == 0 and TM == CK and CK % TQ == 0 and CK == 2 ** LOG2_CK
    c_pad = jnp.zeros((SUBLANES, D), F32).at[:B].set(c)
    grp = jnp.arange(CONV_WIDTH) // GROUP_DIM
    gmat = jnp.where(grp[:, None] == grp[None, :], 1.0 / GROUP_DIM, 0.0).astype(BF16)
    col = lambda w, a, n: w[:, a:a + n]
    o_q = 4 * CONV_WIDTH
    o_k, o_v, o_az = o_q + ATTN_WIDTH, o_q + 2 * ATTN_WIDTH, o_q + 3 * ATTN_WIDTH
    o_qi = o_q + 4 * ATTN_WIDTH
    o_ki = o_qi + IDX_HEADS * IDX_DIM
    o_wi = o_ki + IDX_DIM
    wl = w_in[0]
    w1 = jnp.concatenate(
        [col(wl, 0, 4 * CONV_WIDTH), col(wl, o_q, ATTN_WIDTH), col(wl, o_k, ATTN_WIDTH),
         col(wl, o_az, ATTN_WIDTH), col(wl, o_qi, IDX_HEADS * IDX_DIM),
         col(wl, o_ki, IDX_DIM), col(wl, o_ki, IDX_DIM)], axis=1).astype(BF16)
    w2t = jnp.concatenate(
        [col(wl, o_v, ATTN_WIDTH).T, col(wl, o_wi, IDX_HEADS).T,
         jnp.zeros((_W2_ROWS - ATTN_WIDTH - IDX_HEADS, D), F32)], axis=0).astype(BF16)
    mod = _modulation(c_pad, w_ada[0], b_ada[0][None, :])[:B]
    mod3 = mod.reshape(B, 3, D)
    dup = lambda v: jnp.concatenate([v, v])[None, :]
    yc, q, k, gz, qi, ki2, vt, wit = _projection(
        x, mod3, w1, w2t, conv_w[0], conv_b[0][None, :],
        dup(idx_k_gain[0]), dup(idx_k_bias[0]),
        mix_norm_gain[0][None, :CONV_WIDTH], gmat)
    return _attention(x, yc, gz, q, qi, wit, k, vt, ki2, mod3[:, 2:3, :],
                      w_out[0].astype(BF16), mix_norm_gain[0][None, CONV_WIDTH:],
                      final_gain[None, :])
```

```python
import functools

import numpy as np
import jax
import jax.numpy as jnp
from jax import lax
from jax.experimental import pallas as pl
from jax.experimental.pallas import tpu as pltpu

F32 = jnp.float32
BF16 = jnp.bfloat16
I32 = jnp.int32
I16 = jnp.int16

D_MODEL = 1024
CONV_WIDTH = 512
CONV_K = 3
ATTN_HEADS = 8
HEAD_DIM = 64
ATTN_WIDTH = ATTN_HEADS * HEAD_DIM
IDX_HEADS = 8
IDX_DIM = 64
TOPK_MAX = 256
GROUP_DIM = 64
NORM_EPS = 1e-6
NEG_INF = -1e30

LANES = 128
SUBLANES = 8
PACK16 = 2 * SUBLANES
VT_ROWS = HEAD_DIM + PACK16
VMEM_LIMIT_BYTES = 56 * 1024 * 1024

TM = 512
TQ = 256
CK = 512
LOG2_CK = 9
KT = 256
N_PAIRS = ATTN_HEADS // 2

I16_MIN = -(2 ** 15)
NO_TIE = 2 ** 15 - 1
NEG_INF_BF16 = float(np.asarray(NEG_INF, dtype=BF16).astype(np.float32))

NT_DIMS = (((1,), (1,)), ((), ()))


def _alibi_slopes(n_heads):
    return [2.0 ** (-8.0 * (i + 1) / n_heads) for i in range(n_heads)]


def _mod_kernel(c_ref, w_ref, b_ref, o_ref):
    c_act = jax.nn.silu(c_ref[...])
    o_ref[...] = jnp.dot(c_act.astype(BF16), w_ref[...].astype(BF16),
                         preferred_element_type=F32) + b_ref[...]


def _modulation(c_pad, w_ada, b_ada):
    rows, d = c_pad.shape
    n_out = w_ada.shape[1]
    return pl.pallas_call(
        _mod_kernel,
        grid=(n_out // d,),
        in_specs=[
            pl.BlockSpec((rows, d), lambda j: (0, 0)),
            pl.BlockSpec((d, d), lambda j: (0, j)),
            pl.BlockSpec((1, d), lambda j: (0, j)),
        ],
        out_specs=pl.BlockSpec((rows, d), lambda j: (0, j)),
        out_shape=jax.ShapeDtypeStruct((rows, n_out), F32),
        compiler_params=pltpu.CompilerParams(
            dimension_semantics=("arbitrary",), vmem_limit_bytes=VMEM_LIMIT_BYTES),
        name="adaln_mod",
    )(c_pad, w_ada, b_ada)


_C_CONV = 0
_C_Q = 4 * CONV_WIDTH
_C_K = _C_Q + ATTN_WIDTH
_C_AZ = _C_K + ATTN_WIDTH
_C_QI = _C_AZ + ATTN_WIDTH
_C_KI = _C_QI + IDX_HEADS * IDX_DIM
_W1_COLS = _C_KI + 2 * IDX_DIM
_W2_ROWS = ATTN_WIDTH + 16


def _proj_kernel(x_ref, mod_ref, w1_ref, w2t_ref, cw_ref, cb_ref, lng_ref, lnb_ref,
                 gconv_ref, gmat_ref,
                 yc_ref, q_ref, k_ref, gz_ref, qi_ref, ki2_ref, vt_ref, wit_ref,
                 u_scr):
    i = pl.program_id(1)
    x = x_ref[...]
    ms = jnp.mean(x * x, axis=-1, keepdims=True)
    xn = x * lax.rsqrt(ms + NORM_EPS)
    shift = mod_ref[0:1, :]
    scale = mod_ref[1:2, :]
    h = (xn * (1.0 + scale) + shift).astype(BF16)

    def proj(c0, width):
        return jnp.dot(h, w1_ref[:, c0:c0 + width], preferred_element_type=F32)

    pc = proj(_C_CONV, 4 * CONV_WIDTH)
    b_gate = pc[:, 0:CONV_WIDTH]
    c_gate = pc[:, CONV_WIDTH:2 * CONV_WIDTH]
    x_in = pc[:, 2 * CONV_WIDTH:3 * CONV_WIDTH]
    z_conv = pc[:, 3 * CONV_WIDTH:4 * CONV_WIDTH]
    u = c_gate * x_in

    @pl.when(i == 0)
    def _():
        u_scr[0:SUBLANES, :] = jnp.zeros((SUBLANES, CONV_WIDTH), F32)

    u_scr[SUBLANES:SUBLANES + TM, :] = u
    u_m1 = u_scr[SUBLANES - 1:SUBLANES - 1 + TM, :]
    u_m2 = u_scr[SUBLANES - 2:SUBLANES - 2 + TM, :]
    y = cb_ref[...] + cw_ref[0:1, :] * u_m2
    y = y + cw_ref[1:2, :] * u_m1
    y = y + cw_ref[2:3, :] * u
    conv_out = b_gate * y
    u_scr[0:SUBLANES, :] = u_scr[TM:TM + SUBLANES, :]

    q_ref[...] = (proj(_C_Q, ATTN_WIDTH) * (HEAD_DIM ** -0.5)).astype(BF16)
    k_ref[...] = proj(_C_K, ATTN_WIDTH).astype(BF16)

    sq = conv_out * conv_out
    sq_hi = sq.astype(BF16)
    sq_lo = (sq - sq_hi.astype(F32)).astype(BF16)
    gms = (jnp.dot(sq_hi, gmat_ref[...], preferred_element_type=F32)
           + jnp.dot(sq_lo, gmat_ref[...], preferred_element_type=F32))
    yc = conv_out * lax.rsqrt(gms + NORM_EPS) * gconv_ref[...] * jax.nn.silu(z_conv)
    yc_ref[...] = yc.astype(BF16)

    gz_ref[...] = jax.nn.silu(proj(_C_AZ, ATTN_WIDTH))
    qi_ref[...] = proj(_C_QI, IDX_HEADS * IDX_DIM).astype(BF16)

    pki = proj(_C_KI, 2 * IDX_DIM)
    mu = jnp.mean(pki, axis=-1, keepdims=True)
    var = jnp.mean(jnp.square(pki - mu), axis=-1, keepdims=True)
    ki = (pki - mu) * lax.rsqrt(var + NORM_EPS) * lng_ref[...] + lnb_ref[...]
    ki2_ref[...] = ki.astype(BF16)

    pt = lax.dot_general(w2t_ref[...], h, NT_DIMS, preferred_element_type=F32)
    ones_rows = jnp.where(lax.broadcasted_iota(I32, (PACK16, TM), 0) == 0, 1.0, 0.0).astype(BF16)
    for hd in range(ATTN_HEADS):
        vt_ref[hd * VT_ROWS:hd * VT_ROWS + HEAD_DIM, :] = pt[
            hd * HEAD_DIM:(hd + 1) * HEAD_DIM, :].astype(BF16)
        vt_ref[hd * VT_ROWS + HEAD_DIM:(hd + 1) * VT_ROWS, :] = ones_rows
    idx_scale = (IDX_DIM ** -0.5) * (IDX_HEADS ** -0.5)
    wit_ref[...] = pt[ATTN_WIDTH:ATTN_WIDTH + IDX_HEADS, :] * idx_scale


def _projection(x, mod3, w1, w2t, conv_w, conv_b, lng2, lnb2, gconv, gmat):
    B, S, D = x.shape
    n_t = S // TM
    const2 = lambda b, i: (0, 0)
    out_shapes = (
        jax.ShapeDtypeStruct((B, S, CONV_WIDTH), BF16),
        jax.ShapeDtypeStruct((B, S, ATTN_WIDTH), BF16),
        jax.ShapeDtypeStruct((B, S, ATTN_WIDTH), BF16),
        jax.ShapeDtypeStruct((B, S, ATTN_WIDTH), F32),
        jax.ShapeDtypeStruct((B, S, IDX_HEADS * IDX_DIM), BF16),
        jax.ShapeDtypeStruct((B, S, 2 * IDX_DIM), BF16),
        jax.ShapeDtypeStruct((B, n_t, ATTN_HEADS * VT_ROWS, TM), BF16),
        jax.ShapeDtypeStruct((B, IDX_HEADS, S), F32),
    )
    tok = lambda w: pl.BlockSpec((None, TM, w), lambda b, i: (b, i, 0))
    return pl.pallas_call(
        _proj_kernel,
        grid=(B, n_t),
        in_specs=[
            pl.BlockSpec((None, TM, D), lambda b, i: (b, i, 0)),
            pl.BlockSpec((None, 3, D), lambda b, i: (b, 0, 0)),
            pl.BlockSpec((D, _W1_COLS), const2),
            pl.BlockSpec((_W2_ROWS, D), const2),
            pl.BlockSpec((CONV_K, CONV_WIDTH), const2),
            pl.BlockSpec((1, CONV_WIDTH), const2),
            pl.BlockSpec((1, 2 * IDX_DIM), const2),
            pl.BlockSpec((1, 2 * IDX_DIM), const2),
            pl.BlockSpec((1, CONV_WIDTH), const2),
            pl.BlockSpec((CONV_WIDTH, CONV_WIDTH), const2),
        ],
        out_specs=(
            tok(CONV_WIDTH), tok(ATTN_WIDTH), tok(ATTN_WIDTH), tok(ATTN_WIDTH),
            tok(IDX_HEADS * IDX_DIM), tok(2 * IDX_DIM),
            pl.BlockSpec((None, None, ATTN_HEADS * VT_ROWS, TM), lambda b, i: (b, i, 0, 0)),
            pl.BlockSpec((None, IDX_HEADS, TM), lambda b, i: (b, 0, i)),
        ),
        out_shape=out_shapes,
        scratch_shapes=[pltpu.VMEM((TM + 2 * SUBLANES, CONV_WIDTH), F32)],
        compiler_params=pltpu.CompilerParams(
            dimension_semantics=("arbitrary", "arbitrary"),
            vmem_limit_bytes=VMEM_LIMIT_BYTES),
        name="proj_conv",
    )(x, mod3, w1, w2t, conv_w, conv_b, lng2, lnb2, gconv, gmat)


def _tree(op, xs):
    xs = list(xs)
    while len(xs) > 1:
        xs = [op(xs[i], xs[i + 1]) if i + 1 < len(xs) else xs[i]
              for i in range(0, len(xs), 2)]
    return xs[0]


def _fold_rows(op, x, rows):
    return _tree(op, [x[r:r + rows] for r in range(0, x.shape[0], rows)])


def _attn_kernel(topk, seq_len,
                 x_ref, yc_ref, gz_ref, q_ref, qi_ref, wit_ref, qin_ref, witn_ref,
                 k_ref, vt_ref, ki2_ref, gate_ref, wout_ref, gattn_ref, gfin_ref,
                 o_ref,
                 sc_scr, sb_scr, accn_scr, tie_scr, mb_scr, lga_scr, lgb_scr, acc_scr, m_scr, l_scr, pos_scr,
                 qiw_scr, qw_scr):
    qb = pl.program_id(1)
    t0 = qb * TQ
    n_chunks = lax.shift_right_logical(t0 + (TQ - 1), LOG2_CK) + 1
    slopes = _alibi_slopes(ATTN_HEADS)

    def chunk_loop(body, init):
        return lax.fori_loop(
            0, n_chunks, lambda c, carry: body(c, pl.multiple_of(c * CK, CK), carry), init)

    lane_k = lax.broadcasted_iota(I32, (CK, LANES), 1)
    row_k = lax.broadcasted_iota(I32, (CK, LANES), 0)

    @pl.when(qb == 0)
    def _():
        def body(c, carry):
            base = pl.multiple_of(c * CK, CK)
            s = base + row_k
            val = jnp.where(lane_k == 0, lax.shift_right_logical(s, 6),
                            jnp.where(lane_k == 1, s & 63,
                                      jnp.where(lane_k <= 3, 1, 0)))
            pos_scr[pl.ds(base, CK), :] = val.astype(F32).astype(BF16)
            return carry
        lax.fori_loop(0, seq_len // CK, body, 0)

    lane = lax.broadcasted_iota(I32, (TQ, LANES), 1)
    t_row = t0 + lax.broadcasted_iota(I32, (TQ, LANES), 0)
    t_hi = lax.shift_right_logical(t_row, 6).astype(F32)
    t_lo = (t_row & 63).astype(F32)
    lo_half = lane < HEAD_DIM
    for j in range(N_PAIRS):
        pair = q_ref[:, j * LANES:(j + 1) * LANES]
        zero = jnp.zeros_like(pair)
        for e in range(2):
            sl = slopes[2 * j + e]
            rows = slice(e * TQ, (e + 1) * TQ)
            keep = lo_half if e == 0 else jnp.logical_not(lo_half)
            qw_scr[j, rows, 0:LANES] = jnp.where(keep, pair, zero)
            pcol = jnp.where(lane == 0, 64.0 * sl,
                             jnp.where(lane == 1, sl,
                                       jnp.where(lane == 2, (-64.0 * sl) * t_hi,
                                                 jnp.where(lane == 3, (-sl) * t_lo, 0.0))))
            qw_scr[j, rows, LANES:2 * LANES] = pcol.astype(BF16)

    row = lax.broadcasted_iota(I32, (CK, TQ), 0)
    q_lane = lax.broadcasted_iota(I32, (CK, TQ), 1)
    t_idx = t0 + q_lane

    def build_qiw(qi_src):
        for j in range(N_PAIRS):
            pair = qi_src[:, j * LANES:(j + 1) * LANES]
            zero = jnp.zeros_like(pair)
            qiw_scr[j, 0:TQ, :] = jnp.where(lo_half, pair, zero)
            qiw_scr[j, TQ:2 * TQ, :] = jnp.where(lo_half, zero, pair)

    def score_pair(c, j, wit_src, blk_t0):
        base = pl.multiple_of(c * CK, CK)
        d = lax.dot_general(ki2_ref[pl.ds(base, CK), :], qiw_scr[j], NT_DIMS,
                            preferred_element_type=F32)
        r = jnp.maximum(d, 0.0)
        term = (r[:, 0:TQ] * wit_src[2 * j:2 * j + 1, :]
                + r[:, TQ:2 * TQ] * wit_src[2 * j + 1:2 * j + 2, :])
        if j == 0:
            accn_scr[...] = term
        elif j < N_PAIRS - 1:
            accn_scr[...] += term
        else:
            score = jnp.where(base + row <= blk_t0 + q_lane, accn_scr[...] + term, -jnp.inf)
            sc_scr[pl.ds(base, CK), :] = score
            sb_scr[pl.ds(base, CK), :] = score.astype(BF16)

    def score_chunk(c, wit_src, blk_t0):
        for j in range(N_PAIRS):
            score_pair(c, j, wit_src, blk_t0)

    @pl.when(qb == 0)
    def _():
        build_qiw(qi_ref)

        def body(c, carry):
            score_chunk(c, wit_ref, t0)
            return carry
        lax.fori_loop(0, n_chunks, body, 0)

    nxt_t0 = jnp.minimum(qb + 1, pl.num_programs(1) - 1) * TQ
    nxt_chunks = lax.shift_right_logical(nxt_t0 + (TQ - 1), LOG2_CK) + 1
    build_qiw(qin_ref)

    n_acc = 4
    one16 = jnp.ones((PACK16, TQ), I16)
    zero16 = jnp.zeros((PACK16, TQ), I16)
    zero_row = jnp.zeros((1, TQ), I32)
    n_noncausal = (seq_len - 1) - (t0 + lax.broadcasted_iota(I32, (1, TQ), 1))

    def key_to_f32(key):
        return pltpu.bitcast(jnp.where(key >= 0, key, key ^ 0x7FFFFFFF), F32)

    def count_bf16(cand):
        cand16 = jnp.broadcast_to(cand, (PACK16, TQ)).astype(BF16)

        def body(c, base, accs):
            accs = list(accs)
            blk = sb_scr[pl.ds(base, CK), :]
            for r in range(CK // PACK16):
                hit = blk[r * PACK16:(r + 1) * PACK16] >= cand16
                accs[r % n_acc] = accs[r % n_acc] + jnp.where(hit, one16, zero16)
            return tuple(accs)
        accs = chunk_loop(body, (zero16,) * n_acc)
        tot = ((accs[0] + accs[1]) + (accs[2] + accs[3])).astype(I32)
        return jnp.sum(tot, axis=0, keepdims=True)

    def count_f32(hit_fn):
        zero8 = jnp.zeros((SUBLANES, TQ), I32)

        def body(c, base, accs):
            accs = list(accs)
            blk = sc_scr[pl.ds(base, CK), :]
            for r in range(CK // SUBLANES):
                hit = hit_fn(blk[r * SUBLANES:(r + 1) * SUBLANES], base + r * SUBLANES)
                accs[r % n_acc] = accs[r % n_acc] + jnp.where(hit, 1, 0)
            return tuple(accs)
        accs = chunk_loop(body, (zero8,) * n_acc)
        return jnp.sum((accs[0] + accs[1]) + (accs[2] + accs[3]), axis=0, keepdims=True)

    def bf16_key(u16):
        k16 = u16 + I16_MIN
        return lax.shift_left(k16, 16) + jnp.where(k16 < 0, 0xFFFF, 0)

    def coarse_body(ib, u_thr):
        cand_u = u_thr | lax.shift_left(jnp.int32(1), 15 - ib)
        cand = key_to_f32(bf16_key(cand_u))
        total = count_bf16(cand) + jnp.where(cand <= NEG_INF_BF16, n_noncausal, 0)
        return jnp.where(total >= topk, cand_u, u_thr)

    u_coarse = lax.fori_loop(0, 16, coarse_body, zero_row)
    base_key = bf16_key(u_coarse) - 65536

    def fine_body(ib, off):
        cand_off = off | lax.shift_left(jnp.int32(1), 16 - ib)
        cand = key_to_f32(base_key + cand_off)
        total = (count_f32(lambda sv, _: sv >= cand)
                 + jnp.where(cand <= NEG_INF, n_noncausal, 0))
        return jnp.where(total >= topk, cand_off, off)

    thr = key_to_f32(base_key + lax.fori_loop(0, 17, fine_body, zero_row))

    def mask_body(c, base, acc):
        hit = sc_scr[pl.ds(base, CK), :] >= thr
        mb_scr[pl.ds(base, CK), :] = jnp.where(hit, 0.0, NEG_INF)
        return acc + _fold_rows(jnp.add, jnp.where(hit, 1, 0), SUBLANES)

    cnt_thr = jnp.sum(chunk_loop(mask_body, jnp.zeros((SUBLANES, TQ), I32)),
                      axis=0, keepdims=True)

    def mask_ties():
        def tie_body(c, base, carry):
            tied = sc_scr[pl.ds(base, CK), :] == thr
            tie_scr[pl.ds(base, CK), :] = jnp.where(tied, base + row, NO_TIE).astype(I16)
            return carry
        chunk_loop(tie_body, 0)

        def count_ties_below(idx):
            idx16 = jnp.broadcast_to(idx, (PACK16, TQ)).astype(I16)

            def body(c, base, accs):
                accs = list(accs)
                blk = tie_scr[pl.ds(base, CK), :]
                for r in range(CK // PACK16):
                    hit = blk[r * PACK16:(r + 1) * PACK16] < idx16
                    accs[r % n_acc] = accs[r % n_acc] + jnp.where(hit, one16, zero16)
                return tuple(accs)
            accs = chunk_loop(body, (zero16,) * n_acc)
            tot = ((accs[0] + accs[1]) + (accs[2] + accs[3])).astype(I32)
            return jnp.sum(tot, axis=0, keepdims=True)

        n_tied = count_ties_below(jnp.full((1, TQ), NO_TIE, I32))
        need = topk - (cnt_thr - n_tied)
        n_bits = (seq_len - 1).bit_length()

        def idx_body(ib, j_thr):
            cand = j_thr | lax.shift_left(jnp.int32(1), (n_bits - 1) - ib)
            return jnp.where(count_ties_below(cand) < need, cand, j_thr)

        j_thr = lax.fori_loop(0, n_bits, idx_body, zero_row)

        def body(c, base, carry):
            kv = sc_scr[pl.ds(base, CK), :]
            tie_ok = jnp.where(base + row <= j_thr, 0.0, NEG_INF)
            mb_scr[pl.ds(base, CK), :] = jnp.where(
                kv > thr, 0.0, jnp.where(kv == thr, tie_ok, NEG_INF))
            return carry
        chunk_loop(body, 0)

    lax.cond(jnp.max(cnt_thr) > topk, mask_ties, lambda: None)

    def logits_pair(c, j, dst):
        base = pl.multiple_of(c * CK, CK)
        mb = mb_scr[pl.ds(base, CK), :]
        lhs = jnp.concatenate([k_ref[pl.ds(base, CK), j * LANES:(j + 1) * LANES],
                               pos_scr[pl.ds(base, CK), :]], axis=1)
        lg = lax.dot_general(lhs, qw_scr[j], NT_DIMS, preferred_element_type=F32)
        for e in range(2):
            dst[2 * j + e] = lg[:, e * TQ:(e + 1) * TQ] + mb

    def logits_chunk(c, dst):
        for j in range(N_PAIRS):
            logits_pair(c, j, dst)

    def softmax_heads(c, src, heads):
        for hd in heads:
            m_old = m_scr[hd:hd + 1, :]
            slab_max = _tree(jnp.maximum, [
                _fold_rows(jnp.maximum, src[hd, r:r + KT // 4, :], SUBLANES)
                for r in range(0, CK, KT // 4)])
            m_new = jnp.maximum(m_old, jnp.max(slab_max, axis=0, keepdims=True))
            alpha = jnp.exp(m_old - m_new)
            pv = None
            for kt in range(0, CK, KT):
                p = jnp.exp(src[hd, kt:kt + KT, :] - m_new)
                d = jnp.dot(vt_ref[c, hd * VT_ROWS:(hd + 1) * VT_ROWS, kt:kt + KT],
                            p.astype(BF16), preferred_element_type=F32)
                pv = d if pv is None else pv + d
            l_scr[hd:hd + 1, :] = alpha * l_scr[hd:hd + 1, :] + pv[HEAD_DIM:HEAD_DIM + 1, :]
            m_scr[hd:hd + 1, :] = m_new
            rows = slice(hd * HEAD_DIM, (hd + 1) * HEAD_DIM)
            acc_scr[rows, :] = alpha * acc_scr[rows, :] + pv[0:HEAD_DIM, :]

    m_scr[...] = jnp.full((ATTN_HEADS, TQ), NEG_INF, F32)
    l_scr[...] = jnp.zeros((ATTN_HEADS, TQ), F32)
    acc_scr[...] = jnp.zeros((ATTN_WIDTH, TQ), F32)
    logits_chunk(0, lga_scr)
    last = n_chunks - 1

    def attn_half(c, src, dst):
        c_next = jnp.minimum(c + 1, last)
        for j in range(N_PAIRS):
            logits_pair(c_next, j, dst)
            score_pair(c, j, witn_ref, nxt_t0)
            softmax_heads(c, src, (2 * j, 2 * j + 1))

    def attn_body(t, carry):
        c = 2 * t
        attn_half(c, lga_scr, lgb_scr)

        @pl.when(c < last)
        def _():
            attn_half(c + 1, lgb_scr, lga_scr)
        return carry

    lax.fori_loop(0, lax.shift_right_logical(n_chunks + 1, 1), attn_body, 0)

    @pl.when(nxt_chunks > n_chunks)
    def _():
        score_chunk(n_chunks, witn_ref, nxt_t0)

    normed = []
    for hd in range(ATTN_HEADS):
        rows = slice(hd * HEAD_DIM, (hd + 1) * HEAD_DIM)
        o = acc_scr[rows, :] / l_scr[hd:hd + 1, :]
        ms = jnp.mean(o * o, axis=0, keepdims=True)
        normed.append(o * lax.rsqrt(ms + NORM_EPS))
    attn_t = jnp.concatenate(normed, axis=0)
    ya = attn_t.T * gattn_ref[...] * gz_ref[...]
    upd = (jnp.dot(yc_ref[...], wout_ref[0:CONV_WIDTH, :], preferred_element_type=F32)
           + jnp.dot(ya.astype(BF16), wout_ref[CONV_WIDTH:, :],
                     preferred_element_type=F32))
    xr = x_ref[...] + gate_ref[...] * upd
    ms = jnp.mean(xr * xr, axis=-1, keepdims=True)
    o_ref[...] = xr * lax.rsqrt(ms + NORM_EPS) * gfin_ref[...]


def _attention(x, yc, gz, q, qi, wit, k, vt, ki2, gate, wout, gattn, gfin):
    B, S, D = x.shape
    topk = min(TOPK_MAX, S // 4)
    n_q = S // TQ
    tok = lambda w: pl.BlockSpec((None, TQ, w), lambda b, i: (b, i, 0))
    const2 = lambda b, i: (0, 0)
    once = dict(pipeline_mode=pl.Buffered(1))
    nxt_tok = lambda b, i: (b, jnp.minimum(i + 1, n_q - 1), 0)
    nxt_row = lambda b, i: (b, 0, jnp.minimum(i + 1, n_q - 1))
    return pl.pallas_call(
        functools.partial(_attn_kernel, topk, S),
        grid=(B, n_q),
        in_specs=[
            tok(D),
            tok(CONV_WIDTH),
            tok(ATTN_WIDTH),
            tok(ATTN_WIDTH),
            tok(IDX_HEADS * IDX_DIM),
            pl.BlockSpec((None, IDX_HEADS, TQ), lambda b, i: (b, 0, i)),
            pl.BlockSpec((None, TQ, IDX_HEADS * IDX_DIM), nxt_tok),
            pl.BlockSpec((None, IDX_HEADS, TQ), nxt_row),
            pl.BlockSpec((None, S, ATTN_WIDTH), lambda b, i: (b, 0, 0), **once),
            pl.BlockSpec((None, S // CK, ATTN_HEADS * VT_ROWS, CK), lambda b, i: (b, 0, 0, 0),
                         **once),
            pl.BlockSpec((None, S, 2 * IDX_DIM), lambda b, i: (b, 0, 0), **once),
            pl.BlockSpec((None, 1, D), lambda b, i: (b, 0, 0)),
            pl.BlockSpec((D, D), const2, **once),
            pl.BlockSpec((1, ATTN_WIDTH), const2),
            pl.BlockSpec((1, D), const2),
        ],
        out_specs=tok(D),
        out_shape=jax.ShapeDtypeStruct((B, S, D), F32),
        scratch_shapes=[
            pltpu.VMEM((S, TQ), F32),
            pltpu.VMEM((S, TQ), BF16),
            pltpu.VMEM((CK, TQ), F32),
            pltpu.VMEM((S, TQ), I16),
            pltpu.VMEM((S, TQ), F32),
            pltpu.VMEM((ATTN_HEADS, CK, TQ), F32),
            pltpu.VMEM((ATTN_HEADS, CK, TQ), F32),
            pltpu.VMEM((ATTN_WIDTH, TQ), F32),
            pltpu.VMEM((ATTN_HEADS, TQ), F32),
            pltpu.VMEM((ATTN_HEADS, TQ), F32),
            pltpu.VMEM((S, LANES), BF16),
            pltpu.VMEM((N_PAIRS, 2 * TQ, LANES), BF16),
            pltpu.VMEM((N_PAIRS, 2 * TQ, 2 * LANES), BF16),
        ],
        compiler_params=pltpu.CompilerParams(
            dimension_semantics=("arbitrary", "arbitrary"),
            vmem_limit_bytes=VMEM_LIMIT_BYTES),
        name="dsa_attn_out",
    )(x, yc, gz, q, qi, wit, qi, wit, k, vt, ki2, gate, wout, gattn, gfin)


def kernel(x, c, w_ada, b_ada, w_in, conv_w, conv_b, idx_k_gain, idx_k_bias,
           mix_norm_gain, w_out, final_gain):
    B, S, D = x.shape
    assert w_ada.shape[0] == 1 and B <= SUBLANES
    assert D == D_MODEL and S % TM == 0 and TM == CK and CK % TQ == 0 and CK == 2 ** LOG2_CK
    assert S <= NO_TIE
    c_pad = jnp.zeros((SUBLANES, D), F32).at[:B].set(c)
    grp = jnp.arange(CONV_WIDTH) // GROUP_DIM
    gmat = jnp.where(grp[:, None] == grp[None, :], 1.0 / GROUP_DIM, 0.0).astype(BF16)
    col = lambda w, a, n: w[:, a:a + n]
    o_q = 4 * CONV_WIDTH
    o_k, o_v, o_az = o_q + ATTN_WIDTH, o_q + 2 * ATTN_WIDTH, o_q + 3 * ATTN_WIDTH
    o_qi = o_q + 4 * ATTN_WIDTH
    o_ki = o_qi + IDX_HEADS * IDX_DIM
    o_wi = o_ki + IDX_DIM
    wl = w_in[0]
    w1 = jnp.concatenate(
        [col(wl, 0, 4 * CONV_WIDTH), col(wl, o_q, ATTN_WIDTH), col(wl, o_k, ATTN_WIDTH),
         col(wl, o_az, ATTN_WIDTH), col(wl, o_qi, IDX_HEADS * IDX_DIM),
         col(wl, o_ki, IDX_DIM), col(wl, o_ki, IDX_DIM)], axis=1).astype(BF16)
    w2t = jnp.concatenate(
        [col(wl, o_v, ATTN_WIDTH).T, col(wl, o_wi, IDX_HEADS).T,
         jnp.zeros((_W2_ROWS - ATTN_WIDTH - IDX_HEADS, D), F32)], axis=0).astype(BF16)
    mod = _modulation(c_pad, w_ada[0], b_ada[0][None, :])[:B]
    mod3 = mod.reshape(B, 3, D)
    dup = lambda v: jnp.concatenate([v, v])[None, :]
    yc, q, k, gz, qi, ki2, vt, wit = _projection(
        x, mod3, w1, w2t, conv_w[0], conv_b[0][None, :],
        dup(idx_k_gain[0]), dup(idx_k_bias[0]),
        mix_norm_gain[0][None, :CONV_WIDTH], gmat)
    return _attention(x, yc, gz, q, qi, wit, k, vt, ki2, mod3[:, 2:3, :],
                      w_out[0].astype(BF16), mix_norm_gain[0][None, CONV_WIDTH:],
                      final_gain[None, :])
```

```python
import functools

import numpy as np
import jax
import jax.numpy as jnp
from jax import lax
from jax.experimental import pallas as pl
from jax.experimental.pallas import tpu as pltpu

F32 = jnp.float32
BF16 = jnp.bfloat16
I32 = jnp.int32
I16 = jnp.int16

D_MODEL = 1024
CONV_WIDTH = 512
CONV_K = 3
ATTN_HEADS = 8
HEAD_DIM = 64
ATTN_WIDTH = ATTN_HEADS * HEAD_DIM
IDX_HEADS = 8
IDX_DIM = 64
TOPK_MAX = 256
GROUP_DIM = 64
NORM_EPS = 1e-6
NEG_INF = -1e30

LANES = 128
SUBLANES = 8
PACK16 = 2 * SUBLANES
VT_ROWS = HEAD_DIM + PACK16
VMEM_LIMIT_BYTES = 56 * 1024 * 1024

TM = 512
HM = TM // 2
TQ = 256
CK = 512
LOG2_CK = 9
KT = 256
N_PAIRS = ATTN_HEADS // 2

I16_MIN = -(2 ** 15)
NO_TIE = 2 ** 15 - 1
NEG_INF_BF16 = float(np.asarray(NEG_INF, dtype=BF16).astype(np.float32))

NT_DIMS = (((1,), (1,)), ((), ()))


def _alibi_slopes(n_heads):
    return [2.0 ** (-8.0 * (i + 1) / n_heads) for i in range(n_heads)]


def _mod_kernel(c_ref, w_ref, b_ref, o_ref):
    c_act = jax.nn.silu(c_ref[...])
    o_ref[...] = jnp.dot(c_act.astype(BF16), w_ref[...].astype(BF16),
                         preferred_element_type=F32) + b_ref[...]


def _modulation(c_pad, w_ada, b_ada):
    rows, d = c_pad.shape
    n_out = w_ada.shape[1]
    return pl.pallas_call(
        _mod_kernel,
        grid=(n_out // d,),
        in_specs=[
            pl.BlockSpec((rows, d), lambda j: (0, 0)),
            pl.BlockSpec((d, d), lambda j: (0, j)),
            pl.BlockSpec((1, d), lambda j: (0, j)),
        ],
        out_specs=pl.BlockSpec((rows, d), lambda j: (0, j)),
        out_shape=jax.ShapeDtypeStruct((rows, n_out), F32),
        compiler_params=pltpu.CompilerParams(
            dimension_semantics=("arbitrary",), vmem_limit_bytes=VMEM_LIMIT_BYTES),
        name="adaln_mod",
    )(c_pad, w_ada, b_ada)


_C_CONV = 0
_C_Q = 4 * CONV_WIDTH
_C_K = _C_Q + ATTN_WIDTH
_C_AZ = _C_K + ATTN_WIDTH
_C_QI = _C_AZ + ATTN_WIDTH
_C_KI = _C_QI + IDX_HEADS * IDX_DIM
_W1_COLS = _C_KI + 2 * IDX_DIM
_W2_ROWS = ATTN_WIDTH + 16


def _proj_kernel(x_ref, mod_ref, w1_ref, w2t_ref, cw_ref, cb_ref, lng_ref, lnb_ref,
                 gconv_ref, gmat_ref,
                 yc_ref, q_ref, k_ref, gz_ref, qi_ref, ki2_ref, vt_ref, wit_ref,
                 u_scr):
    i = pl.program_id(1)
    shift = mod_ref[0:1, :]
    scale = mod_ref[1:2, :]
    idx_scale = (IDX_DIM ** -0.5) * (IDX_HEADS ** -0.5)
    ones_rows = jnp.where(lax.broadcasted_iota(I32, (PACK16, HM), 0) == 0, 1.0, 0.0).astype(BF16)

    halves = []
    for r0 in (0, HM):
        x = x_ref[r0:r0 + HM, :]
        ms = jnp.mean(x * x, axis=-1, keepdims=True)
        h = (x * lax.rsqrt(ms + NORM_EPS) * (1.0 + scale) + shift).astype(BF16)
        pc = jnp.dot(h, w1_ref[:, _C_CONV:_C_CONV + 4 * CONV_WIDTH],
                     preferred_element_type=F32)
        halves.append((r0, h, pc))

    @pl.when(i == 0)
    def _():
        u_scr[0:SUBLANES, :] = jnp.zeros((SUBLANES, CONV_WIDTH), F32)

    for r0, h, pc in halves:
        rows = slice(r0, r0 + HM)

        def proj(c0, width):
            return jnp.dot(h, w1_ref[:, c0:c0 + width], preferred_element_type=F32)

        b_gate = pc[:, 0:CONV_WIDTH]
        c_gate = pc[:, CONV_WIDTH:2 * CONV_WIDTH]
        x_in = pc[:, 2 * CONV_WIDTH:3 * CONV_WIDTH]
        z_conv = pc[:, 3 * CONV_WIDTH:4 * CONV_WIDTH]
        u = c_gate * x_in
        u_scr[SUBLANES + r0:SUBLANES + r0 + HM, :] = u
        u_m1 = u_scr[SUBLANES - 1 + r0:SUBLANES - 1 + r0 + HM, :]
        u_m2 = u_scr[SUBLANES - 2 + r0:SUBLANES - 2 + r0 + HM, :]
        y = cb_ref[...] + cw_ref[0:1, :] * u_m2
        y = y + cw_ref[1:2, :] * u_m1
        y = y + cw_ref[2:3, :] * u
        conv_out = b_gate * y
        sq = conv_out * conv_out
        sq_hi = sq.astype(BF16)
        sq_lo = (sq - sq_hi.astype(F32)).astype(BF16)
        gms = (jnp.dot(sq_hi, gmat_ref[...], preferred_element_type=F32)
               + jnp.dot(sq_lo, gmat_ref[...], preferred_element_type=F32))
        yc = conv_out * lax.rsqrt(gms + NORM_EPS) * gconv_ref[...] * jax.nn.silu(z_conv)
        yc_ref[rows, :] = yc.astype(BF16)

        q_ref[rows, :] = (proj(_C_Q, ATTN_WIDTH) * (HEAD_DIM ** -0.5)).astype(BF16)
        k_ref[rows, :] = proj(_C_K, ATTN_WIDTH).astype(BF16)
        gz_ref[rows, :] = jax.nn.silu(proj(_C_AZ, ATTN_WIDTH))
        qi_ref[rows, :] = proj(_C_QI, IDX_HEADS * IDX_DIM).astype(BF16)

        pki = proj(_C_KI, 2 * IDX_DIM)
        mu = jnp.mean(pki, axis=-1, keepdims=True)
        var = jnp.mean(jnp.square(pki - mu), axis=-1, keepdims=True)
        ki = (pki - mu) * lax.rsqrt(var + NORM_EPS) * lng_ref[...] + lnb_ref[...]
        ki2_ref[rows, :] = ki.astype(BF16)

        pt = lax.dot_general(w2t_ref[...], h, NT_DIMS, preferred_element_type=F32)
        for hd in range(ATTN_HEADS):
            vt_ref[hd * VT_ROWS:hd * VT_ROWS + HEAD_DIM, rows] = pt[
                hd * HEAD_DIM:(hd + 1) * HEAD_DIM, :].astype(BF16)
            vt_ref[hd * VT_ROWS + HEAD_DIM:(hd + 1) * VT_ROWS, rows] = ones_rows
        wit_ref[:, rows] = pt[ATTN_WIDTH:ATTN_WIDTH + IDX_HEADS, :] * idx_scale

    u_scr[0:SUBLANES, :] = u_scr[TM:TM + SUBLANES, :]


def _projection(x, mod3, w1, w2t, conv_w, conv_b, lng2, lnb2, gconv, gmat):
    B, S, D = x.shape
    n_t = S // TM
    const2 = lambda b, i: (0, 0)
    out_shapes = (
        jax.ShapeDtypeStruct((B, S, CONV_WIDTH), BF16),
        jax.ShapeDtypeStruct((B, S, ATTN_WIDTH), BF16),
        jax.ShapeDtypeStruct((B, S, ATTN_WIDTH), BF16),
        jax.ShapeDtypeStruct((B, S, ATTN_WIDTH), F32),
        jax.ShapeDtypeStruct((B, S, IDX_HEADS * IDX_DIM), BF16),
        jax.ShapeDtypeStruct((B, S, 2 * IDX_DIM), BF16),
        jax.ShapeDtypeStruct((B, n_t, ATTN_HEADS * VT_ROWS, TM), BF16),
        jax.ShapeDtypeStruct((B, IDX_HEADS, S), F32),
    )
    tok = lambda w: pl.BlockSpec((None, TM, w), lambda b, i: (b, i, 0))
    return pl.pallas_call(
        _proj_kernel,
        grid=(B, n_t),
        in_specs=[
            pl.BlockSpec((None, TM, D), lambda b, i: (b, i, 0)),
            pl.BlockSpec((None, 3, D), lambda b, i: (b, 0, 0)),
            pl.BlockSpec((D, _W1_COLS), const2),
            pl.BlockSpec((_W2_ROWS, D), const2),
            pl.BlockSpec((CONV_K, CONV_WIDTH), const2),
            pl.BlockSpec((1, CONV_WIDTH), const2),
            pl.BlockSpec((1, 2 * IDX_DIM), const2),
            pl.BlockSpec((1, 2 * IDX_DIM), const2),
            pl.BlockSpec((1, CONV_WIDTH), const2),
            pl.BlockSpec((CONV_WIDTH, CONV_WIDTH), const2),
        ],
        out_specs=(
            tok(CONV_WIDTH), tok(ATTN_WIDTH), tok(ATTN_WIDTH), tok(ATTN_WIDTH),
            tok(IDX_HEADS * IDX_DIM), tok(2 * IDX_DIM),
            pl.BlockSpec((None, None, ATTN_HEADS * VT_ROWS, TM), lambda b, i: (b, i, 0, 0)),
            pl.BlockSpec((None, IDX_HEADS, TM), lambda b, i: (b, 0, i)),
        ),
        out_shape=out_shapes,
        scratch_shapes=[pltpu.VMEM((TM + 2 * SUBLANES, CONV_WIDTH), F32)],
        compiler_params=pltpu.CompilerParams(
            dimension_semantics=("arbitrary", "arbitrary"),
            vmem_limit_bytes=VMEM_LIMIT_BYTES),
        name="proj_conv",
    )(x, mod3, w1, w2t, conv_w, conv_b, lng2, lnb2, gconv, gmat)


def _tree(op, xs):
    xs = list(xs)
    while len(xs) > 1:
        xs = [op(xs[i], xs[i + 1]) if i + 1 < len(xs) else xs[i]
              for i in range(0, len(xs), 2)]
    return xs[0]


def _fold_rows(op, x, rows):
    return _tree(op, [x[r:r + rows] for r in range(0, x.shape[0], rows)])


def _attn_kernel(topk, seq_len,
                 x_ref, yc_ref, gz_ref, q_ref, qi_ref, wit_ref, qin_ref, witn_ref,
                 k_ref, vt_ref, ki2_ref, gate_ref, wout_ref, gattn_ref, gfin_ref,
                 o_ref,
                 sc_scr, sb_scr, accn_scr, tie_scr, mb_scr, lga_scr, lgb_scr, acc_scr, m_scr, l_scr, pos_scr,
                 qiw_scr, qw_scr):
    qb = pl.program_id(1)
    t0 = qb * TQ
    n_chunks = lax.shift_right_logical(t0 + (TQ - 1), LOG2_CK) + 1
    slopes = _alibi_slopes(ATTN_HEADS)

    def chunk_loop(body, init):
        return lax.fori_loop(
            0, n_chunks, lambda c, carry: body(c, pl.multiple_of(c * CK, CK), carry), init)

    lane_k = lax.broadcasted_iota(I32, (CK, LANES), 1)
    row_k = lax.broadcasted_iota(I32, (CK, LANES), 0)

    @pl.when(qb == 0)
    def _():
        def body(c, carry):
            base = pl.multiple_of(c * CK, CK)
            s = base + row_k
            val = jnp.where(lane_k == 0, lax.shift_right_logical(s, 6),
                            jnp.where(lane_k == 1, s & 63,
                                      jnp.where(lane_k <= 3, 1, 0)))
            pos_scr[pl.ds(base, CK), :] = val.astype(F32).astype(BF16)
            return carry
        lax.fori_loop(0, seq_len // CK, body, 0)

    lane = lax.broadcasted_iota(I32, (TQ, LANES), 1)
    t_row = t0 + lax.broadcasted_iota(I32, (TQ, LANES), 0)
    t_hi = lax.shift_right_logical(t_row, 6).astype(F32)
    t_lo = (t_row & 63).astype(F32)
    pcol = jnp.where(lane == 0, 64.0,
                     jnp.where(lane == 1, 1.0,
                               jnp.where(lane == 2, -64.0 * t_hi,
                                         jnp.where(lane == 3, -t_lo, 0.0))))
    lo_half = lane < HEAD_DIM
    for j in range(N_PAIRS):
        pair = q_ref[:, j * LANES:(j + 1) * LANES]
        zero = jnp.zeros_like(pair)
        for e in range(2):
            rows = slice(e * TQ, (e + 1) * TQ)
            keep = lo_half if e == 0 else jnp.logical_not(lo_half)
            qw_scr[j, rows, 0:LANES] = jnp.where(keep, pair, zero)
            qw_scr[j, rows, LANES:2 * LANES] = (slopes[2 * j + e] * pcol).astype(BF16)

    row = lax.broadcasted_iota(I32, (CK, TQ), 0)
    q_lane = lax.broadcasted_iota(I32, (CK, TQ), 1)
    t_idx = t0 + q_lane

    def build_qiw(qi_src):
        for j in range(N_PAIRS):
            pair = qi_src[:, j * LANES:(j + 1) * LANES]
            zero = jnp.zeros_like(pair)
            qiw_scr[j, 0:TQ, :] = jnp.where(lo_half, pair, zero)
            qiw_scr[j, TQ:2 * TQ, :] = jnp.where(lo_half, zero, pair)

    def score_pair(c, j, wit_src, blk_t0):
        base = pl.multiple_of(c * CK, CK)
        d = lax.dot_general(ki2_ref[pl.ds(base, CK), :], qiw_scr[j], NT_DIMS,
                            preferred_element_type=F32)
        r = jnp.maximum(d, 0.0)
        term = (r[:, 0:TQ] * wit_src[2 * j:2 * j + 1, :]
                + r[:, TQ:2 * TQ] * wit_src[2 * j + 1:2 * j + 2, :])
        if j == 0:
            accn_scr[...] = term
        elif j < N_PAIRS - 1:
            accn_scr[...] += term
        else:
            score = jnp.where(base + row <= blk_t0 + q_lane, accn_scr[...] + term, -jnp.inf)
            sc_scr[pl.ds(base, CK), :] = score
            sb_scr[pl.ds(base, CK), :] = score.astype(BF16)

    def score_chunk(c, wit_src, blk_t0):
        for j in range(N_PAIRS):
            score_pair(c, j, wit_src, blk_t0)

    @pl.when(qb == 0)
    def _():
        build_qiw(qi_ref)

        def body(c, carry):
            score_chunk(c, wit_ref, t0)
            return carry
        lax.fori_loop(0, n_chunks, body, 0)

    nxt_t0 = jnp.minimum(qb + 1, pl.num_programs(1) - 1) * TQ
    nxt_chunks = lax.shift_right_logical(nxt_t0 + (TQ - 1), LOG2_CK) + 1
    build_qiw(qin_ref)

    n_acc = 4
    one16 = jnp.ones((PACK16, TQ), I16)
    zero16 = jnp.zeros((PACK16, TQ), I16)
    zero_row = jnp.zeros((1, TQ), I32)
    n_noncausal = (seq_len - 1) - (t0 + lax.broadcasted_iota(I32, (1, TQ), 1))

    def key_to_f32(key):
        return pltpu.bitcast(jnp.where(key >= 0, key, key ^ 0x7FFFFFFF), F32)

    def count_bf16(cand):
        cand16 = jnp.broadcast_to(cand, (PACK16, TQ)).astype(BF16)

        def body(c, base, accs):
            accs = list(accs)
            blk = sb_scr[pl.ds(base, CK), :]
            for r in range(CK // PACK16):
                hit = blk[r * PACK16:(r + 1) * PACK16] >= cand16
                accs[r % n_acc] = accs[r % n_acc] + jnp.where(hit, one16, zero16)
            return tuple(accs)
        accs = chunk_loop(body, (zero16,) * n_acc)
        tot = ((accs[0] + accs[1]) + (accs[2] + accs[3])).astype(I32)
        return jnp.sum(tot, axis=0, keepdims=True)

    def count_f32(hit_fn):
        zero8 = jnp.zeros((SUBLANES, TQ), I32)

        def body(c, base, accs):
            accs = list(accs)
            blk = sc_scr[pl.ds(base, CK), :]
            for r in range(CK // SUBLANES):
                hit = hit_fn(blk[r * SUBLANES:(r + 1) * SUBLANES], base + r * SUBLANES)
                accs[r % n_acc] = accs[r % n_acc] + jnp.where(hit, 1, 0)
            return tuple(accs)
        accs = chunk_loop(body, (zero8,) * n_acc)
        return jnp.sum((accs[0] + accs[1]) + (accs[2] + accs[3]), axis=0, keepdims=True)

    def bf16_key(u16):
        k16 = u16 + I16_MIN
        return lax.shift_left(k16, 16) + jnp.where(k16 < 0, 0xFFFF, 0)

    def coarse_body(ib, u_thr):
        cand_u = u_thr | lax.shift_left(jnp.int32(1), 15 - ib)
        cand = key_to_f32(bf16_key(cand_u))
        total = count_bf16(cand) + jnp.where(cand <= NEG_INF_BF16, n_noncausal, 0)
        return jnp.where(total >= topk, cand_u, u_thr)

    u_coarse = lax.fori_loop(0, 16, coarse_body, zero_row)
    base_key = bf16_key(u_coarse) - 65536

    def fine_body(ib, off):
        cand_off = off | lax.shift_left(jnp.int32(1), 16 - ib)
        cand = key_to_f32(base_key + cand_off)
        total = (count_f32(lambda sv, _: sv >= cand)
                 + jnp.where(cand <= NEG_INF, n_noncausal, 0))
        return jnp.where(total >= topk, cand_off, off)

    thr = key_to_f32(base_key + lax.fori_loop(0, 17, fine_body, zero_row))

    def mask_body(c, base, acc):
        hit = sc_scr[pl.ds(base, CK), :] >= thr
        mb_scr[pl.ds(base, CK), :] = jnp.where(hit, 0.0, NEG_INF)
        return acc + _fold_rows(jnp.add, jnp.where(hit, 1, 0), SUBLANES)

    cnt_thr = jnp.sum(chunk_loop(mask_body, jnp.zeros((SUBLANES, TQ), I32)),
                      axis=0, keepdims=True)

    def mask_ties():
        def tie_body(c, base, carry):
            tied = sc_scr[pl.ds(base, CK), :] == thr
            tie_scr[pl.ds(base, CK), :] = jnp.where(tied, base + row, NO_TIE).astype(I16)
            return carry
        chunk_loop(tie_body, 0)

        def count_ties_below(idx):
            idx16 = jnp.broadcast_to(idx, (PACK16, TQ)).astype(I16)

            def body(c, base, accs):
                accs = list(accs)
                blk = tie_scr[pl.ds(base, CK), :]
                for r in range(CK // PACK16):
                    hit = blk[r * PACK16:(r + 1) * PACK16] < idx16
                    accs[r % n_acc] = accs[r % n_acc] + jnp.where(hit, one16, zero16)
                return tuple(accs)
            accs = chunk_loop(body, (zero16,) * n_acc)
            tot = ((accs[0] + accs[1]) + (accs[2] + accs[3])).astype(I32)
            return jnp.sum(tot, axis=0, keepdims=True)

        n_tied = count_ties_below(jnp.full((1, TQ), NO_TIE, I32))
        need = topk - (cnt_thr - n_tied)
        n_bits = LOG2_CK + sum((n_chunks > 2 ** e).astype(I32)
                               for e in range((seq_len // CK - 1).bit_length()))

        def idx_body(ib, j_thr):
            cand = j_thr | lax.shift_left(jnp.int32(1), (n_bits - 1) - ib)
            return jnp.where(count_ties_below(cand) < need, cand, j_thr)

        j_thr = lax.fori_loop(0, n_bits, idx_body, zero_row)

        def body(c, base, carry):
            kv = sc_scr[pl.ds(base, CK), :]
            tie_ok = jnp.where(base + row <= j_thr, 0.0, NEG_INF)
            mb_scr[pl.ds(base, CK), :] = jnp.where(
                kv > thr, 0.0, jnp.where(kv == thr, tie_ok, NEG_INF))
            return carry
        chunk_loop(body, 0)

    lax.cond(jnp.max(cnt_thr) > topk, mask_ties, lambda: None)

    def logits_pair(c, j, dst):
        base = pl.multiple_of(c * CK, CK)
        mb = mb_scr[pl.ds(base, CK), :]
        lhs = jnp.concatenate([k_ref[pl.ds(base, CK), j * LANES:(j + 1) * LANES],
                               pos_scr[pl.ds(base, CK), :]], axis=1)
        lg = lax.dot_general(lhs, qw_scr[j], NT_DIMS, preferred_element_type=F32)
        for e in range(2):
            dst[2 * j + e] = lg[:, e * TQ:(e + 1) * TQ] + mb

    def logits_chunk(c, dst):
        for j in range(N_PAIRS):
            logits_pair(c, j, dst)

    def softmax_heads(c, src, heads):
        for hd in heads:
            m_old = m_scr[hd:hd + 1, :]
            slab_max = _tree(jnp.maximum, [
                _fold_rows(jnp.maximum, src[hd, r:r + KT // 4, :], SUBLANES)
                for r in range(0, CK, KT // 4)])
            m_new = jnp.maximum(m_old, jnp.max(slab_max, axis=0, keepdims=True))
            alpha = jnp.exp(m_old - m_new)
            pv = None
            for kt in range(0, CK, KT):
                p = jnp.exp(src[hd, kt:kt + KT, :] - m_new)
                d = jnp.dot(vt_ref[c, hd * VT_ROWS:(hd + 1) * VT_ROWS, kt:kt + KT],
                            p.astype(BF16), preferred_element_type=F32)
                pv = d if pv is None else pv + d
            l_scr[hd:hd + 1, :] = alpha * l_scr[hd:hd + 1, :] + pv[HEAD_DIM:HEAD_DIM + 1, :]
            m_scr[hd:hd + 1, :] = m_new
            rows = slice(hd * HEAD_DIM, (hd + 1) * HEAD_DIM)
            acc_scr[rows, :] = alpha * acc_scr[rows, :] + pv[0:HEAD_DIM, :]

    m_scr[...] = jnp.full((ATTN_HEADS, TQ), NEG_INF, F32)
    l_scr[...] = jnp.zeros((ATTN_HEADS, TQ), F32)
    acc_scr[...] = jnp.zeros((ATTN_WIDTH, TQ), F32)
    logits_chunk(0, lga_scr)
    last = n_chunks - 1

    def attn_half(c, src, dst):
        c_next = jnp.minimum(c + 1, last)
        for j in range(N_PAIRS):
            logits_pair(c_next, j, dst)
            score_pair(c, j, witn_ref, nxt_t0)
            softmax_heads(c, src, (2 * j, 2 * j + 1))

    def attn_body(t, carry):
        c = 2 * t
        attn_half(c, lga_scr, lgb_scr)

        @pl.when(c < last)
        def _():
            attn_half(c + 1, lgb_scr, lga_scr)
        return carry

    lax.fori_loop(0, lax.shift_right_logical(n_chunks + 1, 1), attn_body, 0)

    @pl.when(nxt_chunks > n_chunks)
    def _():
        score_chunk(n_chunks, witn_ref, nxt_t0)

    normed = []
    for hd in range(ATTN_HEADS):
        rows = slice(hd * HEAD_DIM, (hd + 1) * HEAD_DIM)
        o = acc_scr[rows, :] / l_scr[hd:hd + 1, :]
        ms = jnp.mean(o * o, axis=0, keepdims=True)
        normed.append(o * lax.rsqrt(ms + NORM_EPS))
    attn_t = jnp.concatenate(normed, axis=0)
    ya = attn_t.T * gattn_ref[...] * gz_ref[...]
    upd = (jnp.dot(yc_ref[...], wout_ref[0:CONV_WIDTH, :], preferred_element_type=F32)
           + jnp.dot(ya.astype(BF16), wout_ref[CONV_WIDTH:, :],
                     preferred_element_type=F32))
    xr = x_ref[...] + gate_ref[...] * upd
    ms = jnp.mean(xr * xr, axis=-1, keepdims=True)
    o_ref[...] = xr * lax.rsqrt(ms + NORM_EPS) * gfin_ref[...]


def _attention(x, yc, gz, q, qi, wit, k, vt, ki2, gate, wout, gattn, gfin):
    B, S, D = x.shape
    topk = min(TOPK_MAX, S // 4)
    n_q = S // TQ
    tok = lambda w: pl.BlockSpec((None, TQ, w), lambda b, i: (b, i, 0))
    const2 = lambda b, i: (0, 0)
    once = dict(pipeline_mode=pl.Buffered(1))
    nxt_tok = lambda b, i: (b, jnp.minimum(i + 1, n_q - 1), 0)
    nxt_row = lambda b, i: (b, 0, jnp.minimum(i + 1, n_q - 1))
    return pl.pallas_call(
        functools.partial(_attn_kernel, topk, S),
        grid=(B, n_q),
        in_specs=[
            tok(D),
            tok(CONV_WIDTH),
            tok(ATTN_WIDTH),
            tok(ATTN_WIDTH),
            tok(IDX_HEADS * IDX_DIM),
            pl.BlockSpec((None, IDX_HEADS, TQ), lambda b, i: (b, 0, i)),
            pl.BlockSpec((None, TQ, IDX_HEADS * IDX_DIM), nxt_tok),
            pl.BlockSpec((None, IDX_HEADS, TQ), nxt_row),
            pl.BlockSpec((None, S, ATTN_WIDTH), lambda b, i: (b, 0, 0), **once),
            pl.BlockSpec((None, S // CK, ATTN_HEADS * VT_ROWS, CK), lambda b, i: (b, 0, 0, 0),
                         **once),
            pl.BlockSpec((None, S, 2 * IDX_DIM), lambda b, i: (b, 0, 0), **once),
            pl.BlockSpec((None, 1, D), lambda b, i: (b, 0, 0)),
            pl.BlockSpec((D, D), const2, **once),
            pl.BlockSpec((1, ATTN_WIDTH), const2),
            pl.BlockSpec((1, D), const2),
        ],
        out_specs=tok(D),
        out_shape=jax.ShapeDtypeStruct((B, S, D), F32),
        scratch_shapes=[
            pltpu.VMEM((S, TQ), F32),
            pltpu.VMEM((S, TQ), BF16),
            pltpu.VMEM((CK, TQ), F32),
            pltpu.VMEM((S, TQ), I16),
            pltpu.VMEM((S, TQ), F32),
            pltpu.VMEM((ATTN_HEADS, CK, TQ), F32),
            pltpu.VMEM((ATTN_HEADS, CK, TQ), F32),
            pltpu.VMEM((ATTN_WIDTH, TQ), F32),
            pltpu.VMEM((ATTN_HEADS, TQ), F32),
            pltpu.VMEM((ATTN_HEADS, TQ), F32),
            pltpu.VMEM((S, LANES), BF16),
            pltpu.VMEM((N_PAIRS, 2 * TQ, LANES), BF16),
            pltpu.VMEM((N_PAIRS, 2 * TQ, 2 * LANES), BF16),
        ],
        compiler_params=pltpu.CompilerParams(
            dimension_semantics=("arbitrary", "arbitrary"),
            vmem_limit_bytes=VMEM_LIMIT_BYTES),
        name="dsa_attn_out",
    )(x, yc, gz, q, qi, wit, qi, wit, k, vt, ki2, gate, wout, gattn, gfin)


def kernel(x, c, w_ada, b_ada, w_in, conv_w, conv_b, idx_k_gain, idx_k_bias,
           mix_norm_gain, w_out, final_gain):
    B, S, D = x.shape
    assert w_ada.shape[0] == 1 and B <= SUBLANES
    assert D == D_MODEL and S % TM == 0 and TM == CK and CK % TQ == 0 and CK == 2 ** LOG2_CK
    assert S <= NO_TIE
    c_pad = jnp.zeros((SUBLANES, D), F32).at[:B].set(c)
    grp = jnp.arange(CONV_WIDTH) // GROUP_DIM
    gmat = jnp.where(grp[:, None] == grp[None, :], 1.0 / GROUP_DIM, 0.0).astype(BF16)
    col = lambda w, a, n: w[:, a:a + n]
    o_q = 4 * CONV_WIDTH
    o_k, o_v, o_az = o_q + ATTN_WIDTH, o_q + 2 * ATTN_WIDTH, o_q + 3 * ATTN_WIDTH
    o_qi = o_q + 4 * ATTN_WIDTH
    o_ki = o_qi + IDX_HEADS * IDX_DIM
    o_wi = o_ki + IDX_DIM
    wl = w_in[0]
    w1 = jnp.concatenate(
        [col(wl, 0, 4 * CONV_WIDTH), col(wl, o_q, ATTN_WIDTH), col(wl, o_k, ATTN_WIDTH),
         col(wl, o_az, ATTN_WIDTH), col(wl, o_qi, IDX_HEADS * IDX_DIM),
         col(wl, o_ki, IDX_DIM), col(wl, o_ki, IDX_DIM)], axis=1).astype(BF16)
    w2t = jnp.concatenate(
        [col(wl, o_v, ATTN_WIDTH).T, col(wl, o_wi, IDX_HEADS).T,
         jnp.zeros((_W2_ROWS - ATTN_WIDTH - IDX_HEADS, D), F32)], axis=0).astype(BF16)
    mod = _modulation(c_pad, w_ada[0], b_ada[0][None, :])[:B]
    mod3 = mod.reshape(B, 3, D)
    dup = lambda v: jnp.concatenate([v, v])[None, :]
    yc, q, k, gz, qi, ki2, vt, wit = _projection(
        x, mod3, w1, w2t, conv_w[0], conv_b[0][None, :],
        dup(idx_k_gain[0]), dup(idx_k_bias[0]),
        mix_norm_gain[0][None, :CONV_WIDTH], gmat)
    return _attention(x, yc, gz, q, qi, wit, k, vt, ki2, mod3[:, 2:3, :],
                      w_out[0].astype(BF16), mix_norm_gain[0][None, CONV_WIDTH:],
                      final_gain[None, :])
```

```python
import functools

import numpy as np
import jax
import jax.numpy as jnp
from jax import lax
from jax.experimental import pallas as pl
from jax.experimental.pallas import tpu as pltpu

F32 = jnp.float32
BF16 = jnp.bfloat16
I32 = jnp.int32
I16 = jnp.int16

D_MODEL = 1024
CONV_WIDTH = 512
CONV_K = 3
ATTN_HEADS = 8
HEAD_DIM = 64
ATTN_WIDTH = ATTN_HEADS * HEAD_DIM
IDX_HEADS = 8
IDX_DIM = 64
TOPK_MAX = 256
GROUP_DIM = 64
NORM_EPS = 1e-6
NEG_INF = -1e30

LANES = 128
SUBLANES = 8
PACK16 = 2 * SUBLANES
VT_ROWS = HEAD_DIM + PACK16
VMEM_LIMIT_BYTES = 56 * 1024 * 1024

TM = 512
HM = TM // 2
TQ = 256
CK = 512
LOG2_CK = 9
KT = 256
N_PAIRS = ATTN_HEADS // 2

I16_MIN = -(2 ** 15)
NO_TIE = 2 ** 15 - 1
NEG_INF_BF16 = float(np.asarray(NEG_INF, dtype=BF16).astype(np.float32))

NT_DIMS = (((1,), (1,)), ((), ()))


def _alibi_slopes(n_heads):
    return [2.0 ** (-8.0 * (i + 1) / n_heads) for i in range(n_heads)]


def _mod_kernel(c_ref, w_ref, b_ref, o_ref):
    c_act = jax.nn.silu(c_ref[...])
    o_ref[...] = jnp.dot(c_act.astype(BF16), w_ref[...].astype(BF16),
                         preferred_element_type=F32) + b_ref[...]


def _modulation(c_pad, w_ada, b_ada):
    rows, d = c_pad.shape
    n_out = w_ada.shape[1]
    return pl.pallas_call(
        _mod_kernel,
        grid=(n_out // d,),
        in_specs=[
            pl.BlockSpec((rows, d), lambda j: (0, 0)),
            pl.BlockSpec((d, d), lambda j: (0, j)),
            pl.BlockSpec((1, d), lambda j: (0, j)),
        ],
        out_specs=pl.BlockSpec((rows, d), lambda j: (0, j)),
        out_shape=jax.ShapeDtypeStruct((rows, n_out), F32),
        compiler_params=pltpu.CompilerParams(
            dimension_semantics=("arbitrary",), vmem_limit_bytes=VMEM_LIMIT_BYTES),
        name="adaln_mod",
    )(c_pad, w_ada, b_ada)


_C_CONV = 0
_C_Q = 4 * CONV_WIDTH
_C_K = _C_Q + ATTN_WIDTH
_C_AZ = _C_K + ATTN_WIDTH
_C_QI = _C_AZ + ATTN_WIDTH
_C_KI = _C_QI + IDX_HEADS * IDX_DIM
_W1_COLS = _C_KI + 2 * IDX_DIM
_W2_ROWS = ATTN_WIDTH + 16


def _proj_kernel(x_ref, mod_ref, w1_ref, w2t_ref, cw_ref, cb_ref, lng_ref, lnb_ref,
                 gconv_ref, gmat_ref,
                 yc_ref, q_ref, k_ref, gz_ref, qi_ref, ki2_ref, vt_ref, wit_ref,
                 u_scr):
    i = pl.program_id(1)
    shift = mod_ref[0:1, :]
    scale = mod_ref[1:2, :]
    idx_scale = (IDX_DIM ** -0.5) * (IDX_HEADS ** -0.5)
    ones_rows = jnp.where(lax.broadcasted_iota(I32, (PACK16, HM), 0) == 0, 1.0, 0.0).astype(BF16)

    halves = []
    for r0 in (0, HM):
        x = x_ref[r0:r0 + HM, :]
        ms = jnp.mean(x * x, axis=-1, keepdims=True)
        h = (x * lax.rsqrt(ms + NORM_EPS) * (1.0 + scale) + shift).astype(BF16)
        pc = jnp.dot(h, w1_ref[:, _C_CONV:_C_CONV + 4 * CONV_WIDTH],
                     preferred_element_type=F32)
        halves.append((r0, h, pc))

    @pl.when(i == 0)
    def _():
        u_scr[0:SUBLANES, :] = jnp.zeros((SUBLANES, CONV_WIDTH), F32)

    for r0, h, pc in halves:
        rows = slice(r0, r0 + HM)

        def proj(c0, width):
            return jnp.dot(h, w1_ref[:, c0:c0 + width], preferred_element_type=F32)

        b_gate = pc[:, 0:CONV_WIDTH]
        c_gate = pc[:, CONV_WIDTH:2 * CONV_WIDTH]
        x_in = pc[:, 2 * CONV_WIDTH:3 * CONV_WIDTH]
        z_conv = pc[:, 3 * CONV_WIDTH:4 * CONV_WIDTH]
        u = c_gate * x_in
        u_scr[SUBLANES + r0:SUBLANES + r0 + HM, :] = u
        u_m1 = u_scr[SUBLANES - 1 + r0:SUBLANES - 1 + r0 + HM, :]
        u_m2 = u_scr[SUBLANES - 2 + r0:SUBLANES - 2 + r0 + HM, :]
        y = cb_ref[...] + cw_ref[0:1, :] * u_m2
        y = y + cw_ref[1:2, :] * u_m1
        y = y + cw_ref[2:3, :] * u
        conv_out = b_gate * y
        sq = conv_out * conv_out
        sq_hi = sq.astype(BF16)
        sq_lo = (sq - sq_hi.astype(F32)).astype(BF16)
        gms = (jnp.dot(sq_hi, gmat_ref[...], preferred_element_type=F32)
               + jnp.dot(sq_lo, gmat_ref[...], preferred_element_type=F32))
        yc = conv_out * lax.rsqrt(gms + NORM_EPS) * gconv_ref[...] * jax.nn.silu(z_conv)
        yc_ref[rows, :] = yc.astype(BF16)

        q_ref[rows, :] = (proj(_C_Q, ATTN_WIDTH) * (HEAD_DIM ** -0.5)).astype(BF16)
        k_ref[rows, :] = proj(_C_K, ATTN_WIDTH).astype(BF16)
        gz_ref[rows, :] = jax.nn.silu(proj(_C_AZ, ATTN_WIDTH))
        qi_ref[rows, :] = proj(_C_QI, IDX_HEADS * IDX_DIM).astype(BF16)

        pki = proj(_C_KI, 2 * IDX_DIM)
        mu = jnp.mean(pki, axis=-1, keepdims=True)
        var = jnp.mean(jnp.square(pki - mu), axis=-1, keepdims=True)
        ki = (pki - mu) * lax.rsqrt(var + NORM_EPS) * lng_ref[...] + lnb_ref[...]
        ki2_ref[rows, :] = ki.astype(BF16)

        pt = lax.dot_general(w2t_ref[...], h, NT_DIMS, preferred_element_type=F32)
        for hd in range(ATTN_HEADS):
            vt_ref[hd * VT_ROWS:hd * VT_ROWS + HEAD_DIM, rows] = pt[
                hd * HEAD_DIM:(hd + 1) * HEAD_DIM, :].astype(BF16)
            vt_ref[hd * VT_ROWS + HEAD_DIM:(hd + 1) * VT_ROWS, rows] = ones_rows
        wit_ref[:, rows] = pt[ATTN_WIDTH:ATTN_WIDTH + IDX_HEADS, :] * idx_scale

    u_scr[0:SUBLANES, :] = u_scr[TM:TM + SUBLANES, :]


def _projection(x, mod3, w1, w2t, conv_w, conv_b, lng2, lnb2, gconv, gmat):
    B, S, D = x.shape
    n_t = S // TM
    const2 = lambda b, i: (0, 0)
    out_shapes = (
        jax.ShapeDtypeStruct((B, S, CONV_WIDTH), BF16),
        jax.ShapeDtypeStruct((B, S, ATTN_WIDTH), BF16),
        jax.ShapeDtypeStruct((B, S, ATTN_WIDTH), BF16),
        jax.ShapeDtypeStruct((B, S, ATTN_WIDTH), F32),
        jax.ShapeDtypeStruct((B, S, IDX_HEADS * IDX_DIM), BF16),
        jax.ShapeDtypeStruct((B, S, 2 * IDX_DIM), BF16),
        jax.ShapeDtypeStruct((B, n_t, ATTN_HEADS * VT_ROWS, TM), BF16),
        jax.ShapeDtypeStruct((B, IDX_HEADS, S), F32),
    )
    tok = lambda w: pl.BlockSpec((None, TM, w), lambda b, i: (b, i, 0))
    return pl.pallas_call(
        _proj_kernel,
        grid=(B, n_t),
        in_specs=[
            pl.BlockSpec((None, TM, D), lambda b, i: (b, i, 0)),
            pl.BlockSpec((None, 3, D), lambda b, i: (b, 0, 0)),
            pl.BlockSpec((D, _W1_COLS), const2),
            pl.BlockSpec((_W2_ROWS, D), const2),
            pl.BlockSpec((CONV_K, CONV_WIDTH), const2),
            pl.BlockSpec((1, CONV_WIDTH), const2),
            pl.BlockSpec((1, 2 * IDX_DIM), const2),
            pl.BlockSpec((1, 2 * IDX_DIM), const2),
            pl.BlockSpec((1, CONV_WIDTH), const2),
            pl.BlockSpec((CONV_WIDTH, CONV_WIDTH), const2),
        ],
        out_specs=(
            tok(CONV_WIDTH), tok(ATTN_WIDTH), tok(ATTN_WIDTH), tok(ATTN_WIDTH),
            tok(IDX_HEADS * IDX_DIM), tok(2 * IDX_DIM),
            pl.BlockSpec((None, None, ATTN_HEADS * VT_ROWS, TM), lambda b, i: (b, i, 0, 0)),
            pl.BlockSpec((None, IDX_HEADS, TM), lambda b, i: (b, 0, i)),
        ),
        out_shape=out_shapes,
        scratch_shapes=[pltpu.VMEM((TM + 2 * SUBLANES, CONV_WIDTH), F32)],
        compiler_params=pltpu.CompilerParams(
            dimension_semantics=("arbitrary", "arbitrary"),
            vmem_limit_bytes=VMEM_LIMIT_BYTES),
        name="proj_conv",
    )(x, mod3, w1, w2t, conv_w, conv_b, lng2, lnb2, gconv, gmat)


def _tree(op, xs):
    xs = list(xs)
    while len(xs) > 1:
        xs = [op(xs[i], xs[i + 1]) if i + 1 < len(xs) else xs[i]
              for i in range(0, len(xs), 2)]
    return xs[0]


def _fold_rows(op, x, rows):
    return _tree(op, [x[r:r + rows] for r in range(0, x.shape[0], rows)])


def _attn_kernel(topk, seq_len,
                 x_ref, yc_ref, gz_ref, q_ref, qi_ref, wit_ref, qin_ref, witn_ref,
                 k_ref, vt_ref, ki2_ref, gate_ref, wout_ref, gattn_ref, gfin_ref,
                 o_ref,
                 sc_scr, sb_scr, accn_scr, tie_scr, mb_scr, lga_scr, lgb_scr, acc_scr, m_scr, l_scr, pos_scr,
                 qiw_scr, qw_scr):
    qb = pl.program_id(1)
    t0 = qb * TQ
    n_chunks = lax.shift_right_logical(t0 + (TQ - 1), LOG2_CK) + 1
    slopes = _alibi_slopes(ATTN_HEADS)

    def chunk_loop(body, init):
        return lax.fori_loop(
            0, n_chunks, lambda c, carry: body(c, pl.multiple_of(c * CK, CK), carry), init)

    lane_k = lax.broadcasted_iota(I32, (CK, LANES), 1)
    row_k = lax.broadcasted_iota(I32, (CK, LANES), 0)

    @pl.when(qb == 0)
    def _():
        def body(c, carry):
            base = pl.multiple_of(c * CK, CK)
            s = base + row_k
            val = jnp.where(lane_k == 0, lax.shift_right_logical(s, 6),
                            jnp.where(lane_k == 1, s & 63,
                                      jnp.where(lane_k <= 3, 1, 0)))
            pos_scr[pl.ds(base, CK), :] = val.astype(F32).astype(BF16)
            return carry
        lax.fori_loop(0, seq_len // CK, body, 0)

    lane = lax.broadcasted_iota(I32, (TQ, LANES), 1)
    t_row = t0 + lax.broadcasted_iota(I32, (TQ, LANES), 0)
    t_hi = lax.shift_right_logical(t_row, 6).astype(F32)
    t_lo = (t_row & 63).astype(F32)
    pcol = jnp.where(lane == 0, 64.0,
                     jnp.where(lane == 1, 1.0,
                               jnp.where(lane == 2, -64.0 * t_hi,
                                         jnp.where(lane == 3, -t_lo, 0.0))))
    lo_half = lane < HEAD_DIM
    for j in range(N_PAIRS):
        pair = q_ref[:, j * LANES:(j + 1) * LANES]
        zero = jnp.zeros_like(pair)
        for e in range(2):
            rows = slice(e * TQ, (e + 1) * TQ)
            keep = lo_half if e == 0 else jnp.logical_not(lo_half)
            qw_scr[j, rows, 0:LANES] = jnp.where(keep, pair, zero)
            qw_scr[j, rows, LANES:2 * LANES] = (slopes[2 * j + e] * pcol).astype(BF16)

    row = lax.broadcasted_iota(I32, (CK, TQ), 0)
    q_lane = lax.broadcasted_iota(I32, (CK, TQ), 1)
    t_idx = t0 + q_lane

    def build_qiw(qi_src):
        for j in range(N_PAIRS):
            pair = qi_src[:, j * LANES:(j + 1) * LANES]
            zero = jnp.zeros_like(pair)
            qiw_scr[j, 0:TQ, :] = jnp.where(lo_half, pair, zero)
            qiw_scr[j, TQ:2 * TQ, :] = jnp.where(lo_half, zero, pair)

    def score_pair(c, j, wit_src, blk_t0):
        base = pl.multiple_of(c * CK, CK)
        d = lax.dot_general(ki2_ref[pl.ds(base, CK), :], qiw_scr[j], NT_DIMS,
                            preferred_element_type=F32)
        r = jnp.maximum(d, 0.0)
        term = (r[:, 0:TQ] * wit_src[2 * j:2 * j + 1, :]
                + r[:, TQ:2 * TQ] * wit_src[2 * j + 1:2 * j + 2, :])
        if j == 0:
            accn_scr[...] = term
        elif j < N_PAIRS - 1:
            accn_scr[...] += term
        else:
            score = jnp.where(base + row <= blk_t0 + q_lane, accn_scr[...] + term, -jnp.inf)
            sc_scr[pl.ds(base, CK), :] = score
            sb_scr[pl.ds(base, CK), :] = score.astype(BF16)

    def score_chunk(c, wit_src, blk_t0):
        for j in range(N_PAIRS):
            score_pair(c, j, wit_src, blk_t0)

    @pl.when(qb == 0)
    def _():
        build_qiw(qi_ref)

        def body(c, carry):
            score_chunk(c, wit_ref, t0)
            return carry
        lax.fori_loop(0, n_chunks, body, 0)

    nxt_t0 = jnp.minimum(qb + 1, pl.num_programs(1) - 1) * TQ
    nxt_chunks = lax.shift_right_logical(nxt_t0 + (TQ - 1), LOG2_CK) + 1
    build_qiw(qin_ref)

    n_acc = 4
    one16 = jnp.ones((PACK16, TQ), I16)
    zero16 = jnp.zeros((PACK16, TQ), I16)
    zero_row = jnp.zeros((1, TQ), I32)
    n_noncausal = (seq_len - 1) - (t0 + lax.broadcasted_iota(I32, (1, TQ), 1))

    def key_to_f32(key):
        return pltpu.bitcast(jnp.where(key >= 0, key, key ^ 0x7FFFFFFF), F32)

    def count_bf16(cand):
        cand16 = jnp.broadcast_to(cand, (PACK16, TQ)).astype(BF16)

        def body(c, base, accs):
            accs = list(accs)
            blk = sb_scr[pl.ds(base, CK), :]
            for r in range(CK // PACK16):
                hit = blk[r * PACK16:(r + 1) * PACK16] >= cand16
                accs[r % n_acc] = accs[r % n_acc] + jnp.where(hit, one16, zero16)
            return tuple(accs)
        accs = chunk_loop(body, (zero16,) * n_acc)
        tot = ((accs[0] + accs[1]) + (accs[2] + accs[3])).astype(I32)
        return jnp.sum(tot, axis=0, keepdims=True)

    def count_f32(hit_fn):
        zero8 = jnp.zeros((SUBLANES, TQ), I32)

        def body(c, base, accs):
            accs = list(accs)
            blk = sc_scr[pl.ds(base, CK), :]
            for r in range(CK // SUBLANES):
                hit = hit_fn(blk[r * SUBLANES:(r + 1) * SUBLANES], base + r * SUBLANES)
                accs[r % n_acc] = accs[r % n_acc] + jnp.where(hit, 1, 0)
            return tuple(accs)
        accs = chunk_loop(body, (zero8,) * n_acc)
        return jnp.sum((accs[0] + accs[1]) + (accs[2] + accs[3]), axis=0, keepdims=True)

    def bf16_key(u16):
        k16 = u16 + I16_MIN
        return lax.shift_left(k16, 16) + jnp.where(k16 < 0, 0xFFFF, 0)

    def coarse_body(ib, u_thr):
        cand_u = u_thr | lax.shift_left(jnp.int32(1), 15 - ib)
        cand = key_to_f32(bf16_key(cand_u))
        total = count_bf16(cand) + jnp.where(cand <= NEG_INF_BF16, n_noncausal, 0)
        return jnp.where(total >= topk, cand_u, u_thr)

    u_coarse = lax.fori_loop(0, 16, coarse_body, zero_row)
    base_key = bf16_key(u_coarse) - 65536

    def fine_body(ib, off):
        cand_off = off | lax.shift_left(jnp.int32(1), 16 - ib)
        cand = key_to_f32(base_key + cand_off)
        total = (count_f32(lambda sv, _: sv >= cand)
                 + jnp.where(cand <= NEG_INF, n_noncausal, 0))
        return jnp.where(total >= topk, cand_off, off)

    thr = key_to_f32(base_key + lax.fori_loop(0, 17, fine_body, zero_row))

    def mask_body(c, base, acc):
        hit = sc_scr[pl.ds(base, CK), :] >= thr
        mb_scr[pl.ds(base, CK), :] = jnp.where(hit, 0.0, NEG_INF)
        return acc + _fold_rows(jnp.add, jnp.where(hit, 1, 0), SUBLANES)

    cnt_thr = jnp.sum(chunk_loop(mask_body, jnp.zeros((SUBLANES, TQ), I32)),
                      axis=0, keepdims=True)

    def mask_ties():
        def tie_body(c, base, carry):
            tied = sc_scr[pl.ds(base, CK), :] == thr
            tie_scr[pl.ds(base, CK), :] = jnp.where(tied, base + row, NO_TIE).astype(I16)
            return carry
        chunk_loop(tie_body, 0)

        def count_ties_below(idx):
            idx16 = jnp.broadcast_to(idx, (PACK16, TQ)).astype(I16)

            def body(c, base, accs):
                accs = list(accs)
                blk = tie_scr[pl.ds(base, CK), :]
                for r in range(CK // PACK16):
                    hit = blk[r * PACK16:(r + 1) * PACK16] < idx16
                    accs[r % n_acc] = accs[r % n_acc] + jnp.where(hit, one16, zero16)
                return tuple(accs)
            accs = chunk_loop(body, (zero16,) * n_acc)
            tot = ((accs[0] + accs[1]) + (accs[2] + accs[3])).astype(I32)
            return jnp.sum(tot, axis=0, keepdims=True)

        n_tied = count_ties_below(jnp.full((1, TQ), NO_TIE, I32))
        need = topk - (cnt_thr - n_tied)
        n_bits = LOG2_CK + sum((n_chunks > 2 ** e).astype(I32)
                               for e in range((seq_len // CK - 1).bit_length()))

        def idx_body(ib, j_thr):
            cand = j_thr | lax.shift_left(jnp.int32(1), (n_bits - 1) - ib)
            return jnp.where(count_ties_below(cand) < need, cand, j_thr)

        j_thr = lax.fori_loop(0, n_bits, idx_body, zero_row)

        def body(c, base, carry):
            kv = sc_scr[pl.ds(base, CK), :]
            tie_ok = jnp.where(base + row <= j_thr, 0.0, NEG_INF)
            mb_scr[pl.ds(base, CK), :] = jnp.where(
                kv > thr, 0.0, jnp.where(kv == thr, tie_ok, NEG_INF))
            return carry
        chunk_loop(body, 0)

    lax.cond(jnp.max(cnt_thr) > topk, mask_ties, lambda: None)

    def logits_pair(c, j, dst):
        base = pl.multiple_of(c * CK, CK)
        mb = mb_scr[pl.ds(base, CK), :]
        lhs = jnp.concatenate([k_ref[pl.ds(base, CK), j * LANES:(j + 1) * LANES],
                               pos_scr[pl.ds(base, CK), :]], axis=1)
        lg = lax.dot_general(lhs, qw_scr[j], NT_DIMS, preferred_element_type=F32)
        for e in range(2):
            dst[2 * j + e] = lg[:, e * TQ:(e + 1) * TQ] + mb

    def logits_chunk(c, dst):
        for j in range(N_PAIRS):
            logits_pair(c, j, dst)

    def softmax_heads(c, src, heads):
        for hd in heads:
            m_old = m_scr[hd:hd + 1, :]
            slab_max = _tree(jnp.maximum, [
                _fold_rows(jnp.maximum, src[hd, r:r + KT // 4, :], SUBLANES)
                for r in range(0, CK, KT // 4)])
            m_new = jnp.maximum(m_old, jnp.max(slab_max, axis=0, keepdims=True))
            alpha = jnp.exp(m_old - m_new)
            pv = None
            for kt in range(0, CK, KT):
                p = jnp.exp(src[hd, kt:kt + KT, :] - m_new)
                d = jnp.dot(vt_ref[c, hd * VT_ROWS:(hd + 1) * VT_ROWS, kt:kt + KT],
                            p.astype(BF16), preferred_element_type=F32)
                pv = d if pv is None else pv + d
            l_scr[hd:hd + 1, :] = alpha * l_scr[hd:hd + 1, :] + pv[HEAD_DIM:HEAD_DIM + 1, :]
            m_scr[hd:hd + 1, :] = m_new
            rows = slice(hd * HEAD_DIM, (hd + 1) * HEAD_DIM)
            acc_scr[rows, :] = alpha * acc_scr[rows, :] + pv[0:HEAD_DIM, :]

    m_scr[...] = jnp.full((ATTN_HEADS, TQ), NEG_INF, F32)
    l_scr[...] = jnp.zeros((ATTN_HEADS, TQ), F32)
    acc_scr[...] = jnp.zeros((ATTN_WIDTH, TQ), F32)
    logits_chunk(0, lga_scr)
    last = n_chunks - 1

    def attn_half(c, src, dst):
        c_next = jnp.minimum(c + 1, last)
        for j in range(N_PAIRS):
            logits_pair(c_next, j, dst)
            softmax_heads(c, src, (2 * j,))
            score_pair(c, j, witn_ref, nxt_t0)
            softmax_heads(c, src, (2 * j + 1,))

    def attn_body(t, carry):
        c = 2 * t
        attn_half(c, lga_scr, lgb_scr)

        @pl.when(c < last)
        def _():
            attn_half(c + 1, lgb_scr, lga_scr)
        return carry

    lax.fori_loop(0, lax.shift_right_logical(n_chunks + 1, 1), attn_body, 0)

    @pl.when(nxt_chunks > n_chunks)
    def _():
        score_chunk(n_chunks, witn_ref, nxt_t0)

    normed = []
    for hd in range(ATTN_HEADS):
        rows = slice(hd * HEAD_DIM, (hd + 1) * HEAD_DIM)
        o = acc_scr[rows, :] / l_scr[hd:hd + 1, :]
        ms = jnp.mean(o * o, axis=0, keepdims=True)
        normed.append(o * lax.rsqrt(ms + NORM_EPS))
    attn_t = jnp.concatenate(normed, axis=0)
    ya = attn_t.T * gattn_ref[...] * gz_ref[...]
    upd = (jnp.dot(yc_ref[...], wout_ref[0:CONV_WIDTH, :], preferred_element_type=F32)
           + jnp.dot(ya.astype(BF16), wout_ref[CONV_WIDTH:, :],
                     preferred_element_type=F32))
    xr = x_ref[...] + gate_ref[...] * upd
    ms = jnp.mean(xr * xr, axis=-1, keepdims=True)
    o_ref[...] = xr * lax.rsqrt(ms + NORM_EPS) * gfin_ref[...]


def _attention(x, yc, gz, q, qi, wit, k, vt, ki2, gate, wout, gattn, gfin):
    B, S, D = x.shape
    topk = min(TOPK_MAX, S // 4)
    n_q = S // TQ
    tok = lambda w: pl.BlockSpec((None, TQ, w), lambda b, i: (b, i, 0))
    const2 = lambda b, i: (0, 0)
    once = dict(pipeline_mode=pl.Buffered(1))
    nxt_tok = lambda b, i: (b, jnp.minimum(i + 1, n_q - 1), 0)
    nxt_row = lambda b, i: (b, 0, jnp.minimum(i + 1, n_q - 1))
    return pl.pallas_call(
        functools.partial(_attn_kernel, topk, S),
        grid=(B, n_q),
        in_specs=[
            tok(D),
            tok(CONV_WIDTH),
            tok(ATTN_WIDTH),
            tok(ATTN_WIDTH),
            tok(IDX_HEADS * IDX_DIM),
            pl.BlockSpec((None, IDX_HEADS, TQ), lambda b, i: (b, 0, i)),
            pl.BlockSpec((None, TQ, IDX_HEADS * IDX_DIM), nxt_tok),
            pl.BlockSpec((None, IDX_HEADS, TQ), nxt_row),
            pl.BlockSpec((None, S, ATTN_WIDTH), lambda b, i: (b, 0, 0)),
            pl.BlockSpec((None, S // CK, ATTN_HEADS * VT_ROWS, CK), lambda b, i: (b, 0, 0, 0)),
            pl.BlockSpec((None, S, 2 * IDX_DIM), lambda b, i: (b, 0, 0)),
            pl.BlockSpec((None, 1, D), lambda b, i: (b, 0, 0)),
            pl.BlockSpec((D, D), const2, **once),
            pl.BlockSpec((1, ATTN_WIDTH), const2),
            pl.BlockSpec((1, D), const2),
        ],
        out_specs=tok(D),
        out_shape=jax.ShapeDtypeStruct((B, S, D), F32),
        scratch_shapes=[
            pltpu.VMEM((S, TQ), F32),
            pltpu.VMEM((S, TQ), BF16),
            pltpu.VMEM((CK, TQ), F32),
            pltpu.VMEM((S, TQ), I16),
            pltpu.VMEM((S, TQ), F32),
            pltpu.VMEM((ATTN_HEADS, CK, TQ), F32),
            pltpu.VMEM((ATTN_HEADS, CK, TQ), F32),
            pltpu.VMEM((ATTN_WIDTH, TQ), F32),
            pltpu.VMEM((ATTN_HEADS, TQ), F32),
            pltpu.VMEM((ATTN_HEADS, TQ), F32),
            pltpu.VMEM((S, LANES), BF16),
            pltpu.VMEM((N_PAIRS, 2 * TQ, LANES), BF16),
            pltpu.VMEM((N_PAIRS, 2 * TQ, 2 * LANES), BF16),
        ],
        compiler_params=pltpu.CompilerParams(
            dimension_semantics=("arbitrary", "arbitrary"),
            vmem_limit_bytes=VMEM_LIMIT_BYTES),
        name="dsa_attn_out",
    )(x, yc, gz, q, qi, wit, qi, wit, k, vt, ki2, gate, wout, gattn, gfin)


def kernel(x, c, w_ada, b_ada, w_in, conv_w, conv_b, idx_k_gain, idx_k_bias,
           mix_norm_gain, w_out, final_gain):
    B, S, D = x.shape
    assert w_ada.shape[0] == 1 and B <= SUBLANES
    assert D == D_MODEL and S % TM == 0 and TM == CK and CK % TQ == 0 and CK == 2 ** LOG2_CK
    assert S <= NO_TIE
    c_pad = jnp.zeros((SUBLANES, D), F32).at[:B].set(c)
    grp = jnp.arange(CONV_WIDTH) // GROUP_DIM
    gmat = jnp.where(grp[:, None] == grp[None, :], 1.0 / GROUP_DIM, 0.0).astype(BF16)
    col = lambda w, a, n: w[:, a:a + n]
    o_q = 4 * CONV_WIDTH
    o_k, o_v, o_az = o_q + ATTN_WIDTH, o_q + 2 * ATTN_WIDTH, o_q + 3 * ATTN_WIDTH
    o_qi = o_q + 4 * ATTN_WIDTH
    o_ki = o_qi + IDX_HEADS * IDX_DIM
    o_wi = o_ki + IDX_DIM
    wl = w_in[0]
    w1 = jnp.concatenate(
        [col(wl, 0, 4 * CONV_WIDTH), col(wl, o_q, ATTN_WIDTH), col(wl, o_k, ATTN_WIDTH),
         col(wl, o_az, ATTN_WIDTH), col(wl, o_qi, IDX_HEADS * IDX_DIM),
         col(wl, o_ki, IDX_DIM), col(wl, o_ki, IDX_DIM)], axis=1).astype(BF16)
    w2t = jnp.concatenate(
        [col(wl, o_v, ATTN_WIDTH).T, col(wl, o_wi, IDX_HEADS).T,
         jnp.zeros((_W2_ROWS - ATTN_WIDTH - IDX_HEADS, D), F32)], axis=0).astype(BF16)
    mod = _modulation(c_pad, w_ada[0], b_ada[0][None, :])[:B]
    mod3 = mod.reshape(B, 3, D)
    dup = lambda v: jnp.concatenate([v, v])[None, :]
    yc, q, k, gz, qi, ki2, vt, wit = _projection(
        x, mod3, w1, w2t, conv_w[0], conv_b[0][None, :],
        dup(idx_k_gain[0]), dup(idx_k_bias[0]),
        mix_norm_gain[0][None, :CONV_WIDTH], gmat)
    return _attention(x, yc, gz, q, qi, wit, k, vt, ki2, mod3[:, 2:3, :],
                      w_out[0].astype(BF16), mix_norm_gain[0][None, CONV_WIDTH:],
                      final_gain[None, :])
```

```python
import functools

import numpy as np
import jax
import jax.numpy as jnp
from jax import lax
from jax.experimental import pallas as pl
from jax.experimental.pallas import tpu as pltpu

F32 = jnp.float32
BF16 = jnp.bfloat16
I32 = jnp.int32
I16 = jnp.int16

D_MODEL = 1024
CONV_WIDTH = 512
CONV_K = 3
ATTN_HEADS = 8
HEAD_DIM = 64
ATTN_WIDTH = ATTN_HEADS * HEAD_DIM
IDX_HEADS = 8
IDX_DIM = 64
TOPK_MAX = 256
GROUP_DIM = 64
NORM_EPS = 1e-6
NEG_INF = -1e30

LANES = 128
SUBLANES = 8
PACK16 = 2 * SUBLANES
VT_ROWS = HEAD_DIM + PACK16
VMEM_LIMIT_BYTES = 56 * 1024 * 1024

TM = 512
HM = TM // 2
TQ = 256
CK = 512
LOG2_CK = 9
KT = 256
N_PAIRS = ATTN_HEADS // 2

I16_MIN = -(2 ** 15)
NO_TIE = 2 ** 15 - 1
NEG_INF_BF16 = float(np.asarray(NEG_INF, dtype=BF16).astype(np.float32))

NT_DIMS = (((1,), (1,)), ((), ()))


def _alibi_slopes(n_heads):
    return [2.0 ** (-8.0 * (i + 1) / n_heads) for i in range(n_heads)]


def _mod_kernel(c_ref, w_ref, b_ref, o_ref):
    c_act = jax.nn.silu(c_ref[...])
    o_ref[...] = jnp.dot(c_act.astype(BF16), w_ref[...].astype(BF16),
                         preferred_element_type=F32) + b_ref[...]


def _modulation(c_pad, w_ada, b_ada):
    rows, d = c_pad.shape
    n_out = w_ada.shape[1]
    return pl.pallas_call(
        _mod_kernel,
        grid=(n_out // d,),
        in_specs=[
            pl.BlockSpec((rows, d), lambda j: (0, 0)),
            pl.BlockSpec((d, d), lambda j: (0, j)),
            pl.BlockSpec((1, d), lambda j: (0, j)),
        ],
        out_specs=pl.BlockSpec((rows, d), lambda j: (0, j)),
        out_shape=jax.ShapeDtypeStruct((rows, n_out), F32),
        compiler_params=pltpu.CompilerParams(
            dimension_semantics=("arbitrary",), vmem_limit_bytes=VMEM_LIMIT_BYTES),
        name="adaln_mod",
    )(c_pad, w_ada, b_ada)


_C_CONV = 0
_C_Q = 4 * CONV_WIDTH
_C_K = _C_Q + ATTN_WIDTH
_C_AZ = _C_K + ATTN_WIDTH
_C_QI = _C_AZ + ATTN_WIDTH
_C_KI = _C_QI + IDX_HEADS * IDX_DIM
_W1_COLS = _C_KI + 2 * IDX_DIM
_W2_ROWS = ATTN_WIDTH + 16


def _proj_kernel(x_ref, mod_ref, w1_ref, w2t_ref, cw_ref, cb_ref, lng_ref, lnb_ref,
                 gconv_ref, gmat_ref,
                 yc_ref, q_ref, k_ref, gz_ref, qi_ref, ki2_ref, vt_ref, wit_ref,
                 u_scr):
    i = pl.program_id(1)
    shift = mod_ref[0:1, :]
    scale = mod_ref[1:2, :]
    idx_scale = (IDX_DIM ** -0.5) * (IDX_HEADS ** -0.5)
    ones_rows = jnp.where(lax.broadcasted_iota(I32, (PACK16, HM), 0) == 0, 1.0, 0.0).astype(BF16)

    halves = []
    for r0 in (0, HM):
        x = x_ref[r0:r0 + HM, :]
        ms = jnp.mean(x * x, axis=-1, keepdims=True)
        h = (x * lax.rsqrt(ms + NORM_EPS) * (1.0 + scale) + shift).astype(BF16)
        pc = jnp.dot(h, w1_ref[:, _C_CONV:_C_CONV + 4 * CONV_WIDTH],
                     preferred_element_type=F32)
        halves.append((r0, h, pc))

    @pl.when(i == 0)
    def _():
        u_scr[0:SUBLANES, :] = jnp.zeros((SUBLANES, CONV_WIDTH), F32)

    for r0, h, pc in halves:
        rows = slice(r0, r0 + HM)

        def proj(c0, width):
            return jnp.dot(h, w1_ref[:, c0:c0 + width], preferred_element_type=F32)

        b_gate = pc[:, 0:CONV_WIDTH]
        c_gate = pc[:, CONV_WIDTH:2 * CONV_WIDTH]
        x_in = pc[:, 2 * CONV_WIDTH:3 * CONV_WIDTH]
        z_conv = pc[:, 3 * CONV_WIDTH:4 * CONV_WIDTH]
        u = c_gate * x_in
        u_scr[SUBLANES + r0:SUBLANES + r0 + HM, :] = u
        u_m1 = u_scr[SUBLANES - 1 + r0:SUBLANES - 1 + r0 + HM, :]
        u_m2 = u_scr[SUBLANES - 2 + r0:SUBLANES - 2 + r0 + HM, :]
        y = cb_ref[...] + cw_ref[0:1, :] * u_m2
        y = y + cw_ref[1:2, :] * u_m1
        y = y + cw_ref[2:3, :] * u
        conv_out = b_gate * y
        sq = conv_out * conv_out
        sq_hi = sq.astype(BF16)
        sq_lo = (sq - sq_hi.astype(F32)).astype(BF16)
        gms = (jnp.dot(sq_hi, gmat_ref[...], preferred_element_type=F32)
               + jnp.dot(sq_lo, gmat_ref[...], preferred_element_type=F32))
        yc = conv_out * lax.rsqrt(gms + NORM_EPS) * gconv_ref[...] * jax.nn.silu(z_conv)
        yc_ref[rows, :] = yc.astype(BF16)

        q_ref[rows, :] = (proj(_C_Q, ATTN_WIDTH) * (HEAD_DIM ** -0.5)).astype(BF16)
        k_ref[rows, :] = proj(_C_K, ATTN_WIDTH).astype(BF16)
        gz_ref[rows, :] = jax.nn.silu(proj(_C_AZ, ATTN_WIDTH))
        qi_ref[rows, :] = proj(_C_QI, IDX_HEADS * IDX_DIM).astype(BF16)

        pki = proj(_C_KI, 2 * IDX_DIM)
        mu = jnp.mean(pki, axis=-1, keepdims=True)
        var = jnp.mean(jnp.square(pki - mu), axis=-1, keepdims=True)
        ki = (pki - mu) * lax.rsqrt(var + NORM_EPS) * lng_ref[...] + lnb_ref[...]
        ki2_ref[rows, :] = ki.astype(BF16)

        pt = lax.dot_general(w2t_ref[...], h, NT_DIMS, preferred_element_type=F32)
        for hd in range(ATTN_HEADS):
            vt_ref[hd * VT_ROWS:hd * VT_ROWS + HEAD_DIM, rows] = pt[
                hd * HEAD_DIM:(hd + 1) * HEAD_DIM, :].astype(BF16)
            vt_ref[hd * VT_ROWS + HEAD_DIM:(hd + 1) * VT_ROWS, rows] = ones_rows
        wit_ref[:, rows] = pt[ATTN_WIDTH:ATTN_WIDTH + IDX_HEADS, :] * idx_scale

    u_scr[0:SUBLANES, :] = u_scr[TM:TM + SUBLANES, :]


def _projection(x, mod3, w1, w2t, conv_w, conv_b, lng2, lnb2, gconv, gmat):
    B, S, D = x.shape
    n_t = S // TM
    const2 = lambda b, i: (0, 0)
    out_shapes = (
        jax.ShapeDtypeStruct((B, S, CONV_WIDTH), BF16),
        jax.ShapeDtypeStruct((B, S, ATTN_WIDTH), BF16),
        jax.ShapeDtypeStruct((B, S, ATTN_WIDTH), BF16),
        jax.ShapeDtypeStruct((B, S, ATTN_WIDTH), F32),
        jax.ShapeDtypeStruct((B, S, IDX_HEADS * IDX_DIM), BF16),
        jax.ShapeDtypeStruct((B, S, 2 * IDX_DIM), BF16),
        jax.ShapeDtypeStruct((B, n_t, ATTN_HEADS * VT_ROWS, TM), BF16),
        jax.ShapeDtypeStruct((B, IDX_HEADS, S), F32),
    )
    tok = lambda w: pl.BlockSpec((None, TM, w), lambda b, i: (b, i, 0))
    return pl.pallas_call(
        _proj_kernel,
        grid=(B, n_t),
        in_specs=[
            pl.BlockSpec((None, TM, D), lambda b, i: (b, i, 0)),
            pl.BlockSpec((None, 3, D), lambda b, i: (b, 0, 0)),
            pl.BlockSpec((D, _W1_COLS), const2),
            pl.BlockSpec((_W2_ROWS, D), const2),
            pl.BlockSpec((CONV_K, CONV_WIDTH), const2),
            pl.BlockSpec((1, CONV_WIDTH), const2),
            pl.BlockSpec((1, 2 * IDX_DIM), const2),
            pl.BlockSpec((1, 2 * IDX_DIM), const2),
            pl.BlockSpec((1, CONV_WIDTH), const2),
            pl.BlockSpec((CONV_WIDTH, CONV_WIDTH), const2),
        ],
        out_specs=(
            tok(CONV_WIDTH), tok(ATTN_WIDTH), tok(ATTN_WIDTH), tok(ATTN_WIDTH),
            tok(IDX_HEADS * IDX_DIM), tok(2 * IDX_DIM),
            pl.BlockSpec((None, None, ATTN_HEADS * VT_ROWS, TM), lambda b, i: (b, i, 0, 0)),
            pl.BlockSpec((None, IDX_HEADS, TM), lambda b, i: (b, 0, i)),
        ),
        out_shape=out_shapes,
        scratch_shapes=[pltpu.VMEM((TM + 2 * SUBLANES, CONV_WIDTH), F32)],
        compiler_params=pltpu.CompilerParams(
            dimension_semantics=("arbitrary", "arbitrary"),
            vmem_limit_bytes=VMEM_LIMIT_BYTES),
        name="proj_conv",
    )(x, mod3, w1, w2t, conv_w, conv_b, lng2, lnb2, gconv, gmat)


def _tree(op, xs):
    xs = list(xs)
    while len(xs) > 1:
        xs = [op(xs[i], xs[i + 1]) if i + 1 < len(xs) else xs[i]
              for i in range(0, len(xs), 2)]
    return xs[0]


def _fold_rows(op, x, rows):
    return _tree(op, [x[r:r + rows] for r in range(0, x.shape[0], rows)])


def _attn_kernel(topk, seq_len,
                 x_ref, yc_ref, gz_ref, q_ref, qi_ref, wit_ref, qin_ref, witn_ref,
                 k_ref, vt_ref, ki2_ref, gate_ref, wout_ref, gattn_ref, gfin_ref,
                 o_ref,
                 sc_scr, sb_scr, accn_scr, tie_scr, mb_scr, lga_scr, lgb_scr, acc_scr, m_scr, l_scr, pos_scr,
                 qiw_scr, qw_scr):
    qb = pl.program_id(1)
    t0 = qb * TQ
    n_chunks = lax.shift_right_logical(t0 + (TQ - 1), LOG2_CK) + 1
    slopes = _alibi_slopes(ATTN_HEADS)

    def chunk_loop(body, init):
        return lax.fori_loop(
            0, n_chunks, lambda c, carry: body(c, pl.multiple_of(c * CK, CK), carry), init)

    lane_k = lax.broadcasted_iota(I32, (CK, LANES), 1)
    row_k = lax.broadcasted_iota(I32, (CK, LANES), 0)

    @pl.when(qb == 0)
    def _():
        def body(c, carry):
            base = pl.multiple_of(c * CK, CK)
            s = base + row_k
            val = jnp.where(lane_k == 0, lax.shift_right_logical(s, 6),
                            jnp.where(lane_k == 1, s & 63,
                                      jnp.where(lane_k <= 3, 1, 0)))
            pos_scr[pl.ds(base, CK), :] = val.astype(F32).astype(BF16)
            return carry
        lax.fori_loop(0, seq_len // CK, body, 0)

    lane = lax.broadcasted_iota(I32, (TQ, LANES), 1)
    t_row = t0 + lax.broadcasted_iota(I32, (TQ, LANES), 0)
    t_hi = lax.shift_right_logical(t_row, 6).astype(F32)
    t_lo = (t_row & 63).astype(F32)
    pcol = jnp.where(lane == 0, 64.0,
                     jnp.where(lane == 1, 1.0,
                               jnp.where(lane == 2, -64.0 * t_hi,
                                         jnp.where(lane == 3, -t_lo, 0.0))))
    lo_half = lane < HEAD_DIM

    def build_qw(j):
        pair = q_ref[:, j * LANES:(j + 1) * LANES]
        zero = jnp.zeros_like(pair)
        for e in range(2):
            rows = slice(e * TQ, (e + 1) * TQ)
            keep = lo_half if e == 0 else jnp.logical_not(lo_half)
            qw_scr[j, rows, 0:LANES] = jnp.where(keep, pair, zero)
            qw_scr[j, rows, LANES:2 * LANES] = (slopes[2 * j + e] * pcol).astype(BF16)

    row = lax.broadcasted_iota(I32, (CK, TQ), 0)
    q_lane = lax.broadcasted_iota(I32, (CK, TQ), 1)
    t_idx = t0 + q_lane

    def build_qiw(qi_src):
        for j in range(N_PAIRS):
            pair = qi_src[:, j * LANES:(j + 1) * LANES]
            zero = jnp.zeros_like(pair)
            qiw_scr[j, 0:TQ, :] = jnp.where(lo_half, pair, zero)
            qiw_scr[j, TQ:2 * TQ, :] = jnp.where(lo_half, zero, pair)

    def score_pair(c, j, wit_src, blk_t0):
        base = pl.multiple_of(c * CK, CK)
        d = lax.dot_general(ki2_ref[pl.ds(base, CK), :], qiw_scr[j], NT_DIMS,
                            preferred_element_type=F32)
        r = jnp.maximum(d, 0.0)
        term = (r[:, 0:TQ] * wit_src[2 * j:2 * j + 1, :]
                + r[:, TQ:2 * TQ] * wit_src[2 * j + 1:2 * j + 2, :])
        if j == 0:
            accn_scr[...] = term
        elif j < N_PAIRS - 1:
            accn_scr[...] += term
        else:
            score = jnp.where(base + row <= blk_t0 + q_lane, accn_scr[...] + term, -jnp.inf)
            sc_scr[pl.ds(base, CK), :] = score
            sb_scr[pl.ds(base, CK), :] = score.astype(BF16)

    def score_chunk(c, wit_src, blk_t0):
        for j in range(N_PAIRS):
            score_pair(c, j, wit_src, blk_t0)

    @pl.when(qb == 0)
    def _():
        build_qiw(qi_ref)

        def body(c, carry):
            score_chunk(c, wit_ref, t0)
            return carry
        lax.fori_loop(0, n_chunks, body, 0)

    nxt_t0 = jnp.minimum(qb + 1, pl.num_programs(1) - 1) * TQ
    nxt_chunks = lax.shift_right_logical(nxt_t0 + (TQ - 1), LOG2_CK) + 1
    build_qiw(qin_ref)

    n_acc = 4
    one16 = jnp.ones((PACK16, TQ), I16)
    zero16 = jnp.zeros((PACK16, TQ), I16)
    zero_row = jnp.zeros((1, TQ), I32)
    n_noncausal = (seq_len - 1) - (t0 + lax.broadcasted_iota(I32, (1, TQ), 1))

    def key_to_f32(key):
        return pltpu.bitcast(jnp.where(key >= 0, key, key ^ 0x7FFFFFFF), F32)

    def count_bf16(cand):
        cand16 = jnp.broadcast_to(cand, (PACK16, TQ)).astype(BF16)

        def body(c, base, accs):
            accs = list(accs)
            blk = sb_scr[pl.ds(base, CK), :]
            for r in range(CK // PACK16):
                hit = blk[r * PACK16:(r + 1) * PACK16] >= cand16
                accs[r % n_acc] = accs[r % n_acc] + jnp.where(hit, one16, zero16)
            return tuple(accs)
        accs = chunk_loop(body, (zero16,) * n_acc)
        tot = ((accs[0] + accs[1]) + (accs[2] + accs[3])).astype(I32)
        return jnp.sum(tot, axis=0, keepdims=True)

    def count_f32(hit_fn):
        zero8 = jnp.zeros((SUBLANES, TQ), I32)

        def body(c, base, accs):
            accs = list(accs)
            blk = sc_scr[pl.ds(base, CK), :]
            for r in range(CK // SUBLANES):
                hit = hit_fn(blk[r * SUBLANES:(r + 1) * SUBLANES], base + r * SUBLANES)
                accs[r % n_acc] = accs[r % n_acc] + jnp.where(hit, 1, 0)
            return tuple(accs)
        accs = chunk_loop(body, (zero8,) * n_acc)
        return jnp.sum((accs[0] + accs[1]) + (accs[2] + accs[3]), axis=0, keepdims=True)

    def bf16_key(u16):
        k16 = u16 + I16_MIN
        return lax.shift_left(k16, 16) + jnp.where(k16 < 0, 0xFFFF, 0)

    def coarse_body(ib, u_thr):
        cand_u = u_thr | lax.shift_left(jnp.int32(1), 15 - ib)
        cand = key_to_f32(bf16_key(cand_u))
        total = count_bf16(cand) + jnp.where(cand <= NEG_INF_BF16, n_noncausal, 0)
        return jnp.where(total >= topk, cand_u, u_thr)

    u_coarse = lax.fori_loop(0, 16, coarse_body, zero_row)
    base_key = bf16_key(u_coarse) - 65536

    def fine_body(ib, off):
        cand_off = off | lax.shift_left(jnp.int32(1), 16 - ib)
        cand = key_to_f32(base_key + cand_off)
        total = (count_f32(lambda sv, _: sv >= cand)
                 + jnp.where(cand <= NEG_INF, n_noncausal, 0))
        return jnp.where(total >= topk, cand_off, off)

    thr = key_to_f32(base_key + lax.fori_loop(0, 17, fine_body, zero_row))

    def mask_body(c, base, acc):
        hit = sc_scr[pl.ds(base, CK), :] >= thr
        mb_scr[pl.ds(base, CK), :] = jnp.where(hit, 0.0, NEG_INF)
        return acc + _fold_rows(jnp.add, jnp.where(hit, 1, 0), SUBLANES)

    cnt_thr = jnp.sum(chunk_loop(mask_body, jnp.zeros((SUBLANES, TQ), I32)),
                      axis=0, keepdims=True)

    def mask_ties():
        def tie_body(c, base, carry):
            tied = sc_scr[pl.ds(base, CK), :] == thr
            tie_scr[pl.ds(base, CK), :] = jnp.where(tied, base + row, NO_TIE).astype(I16)
            return carry
        chunk_loop(tie_body, 0)

        def count_ties_below(idx):
            idx16 = jnp.broadcast_to(idx, (PACK16, TQ)).astype(I16)

            def body(c, base, accs):
                accs = list(accs)
                blk = tie_scr[pl.ds(base, CK), :]
                for r in range(CK // PACK16):
                    hit = blk[r * PACK16:(r + 1) * PACK16] < idx16
                    accs[r % n_acc] = accs[r % n_acc] + jnp.where(hit, one16, zero16)
                return tuple(accs)
            accs = chunk_loop(body, (zero16,) * n_acc)
            tot = ((accs[0] + accs[1]) + (accs[2] + accs[3])).astype(I32)
            return jnp.sum(tot, axis=0, keepdims=True)

        n_tied = count_ties_below(jnp.full((1, TQ), NO_TIE, I32))
        need = topk - (cnt_thr - n_tied)
        n_bits = LOG2_CK + sum((n_chunks > 2 ** e).astype(I32)
                               for e in range((seq_len // CK - 1).bit_length()))

        def idx_body(ib, j_thr):
            cand = j_thr | lax.shift_left(jnp.int32(1), (n_bits - 1) - ib)
            return jnp.where(count_ties_below(cand) < need, cand, j_thr)

        j_thr = lax.fori_loop(0, n_bits, idx_body, zero_row)

        def body(c, base, carry):
            kv = sc_scr[pl.ds(base, CK), :]
            tie_ok = jnp.where(base + row <= j_thr, 0.0, NEG_INF)
            mb_scr[pl.ds(base, CK), :] = jnp.where(
                kv > thr, 0.0, jnp.where(kv == thr, tie_ok, NEG_INF))
            return carry
        chunk_loop(body, 0)

    lax.cond(jnp.max(cnt_thr) > topk, mask_ties, lambda: None)

    def logits_pair(c, j, dst):
        base = pl.multiple_of(c * CK, CK)
        mb = mb_scr[pl.ds(base, CK), :]
        lhs = jnp.concatenate([k_ref[pl.ds(base, CK), j * LANES:(j + 1) * LANES],
                               pos_scr[pl.ds(base, CK), :]], axis=1)
        lg = lax.dot_general(lhs, qw_scr[j], NT_DIMS, preferred_element_type=F32)
        for e in range(2):
            dst[2 * j + e] = lg[:, e * TQ:(e + 1) * TQ] + mb

    def logits_chunk(c, dst):
        for j in range(N_PAIRS):
            logits_pair(c, j, dst)

    def softmax_heads(c, src, heads):
        for hd in heads:
            m_old = m_scr[hd:hd + 1, :]
            slab_max = _tree(jnp.maximum, [
                _fold_rows(jnp.maximum, src[hd, r:r + KT // 4, :], SUBLANES)
                for r in range(0, CK, KT // 4)])
            m_new = jnp.maximum(m_old, jnp.max(slab_max, axis=0, keepdims=True))
            alpha = jnp.exp(m_old - m_new)
            pv = None
            for kt in range(0, CK, KT):
                p = jnp.exp(src[hd, kt:kt + KT, :] - m_new)
                d = jnp.dot(vt_ref[c, hd * VT_ROWS:(hd + 1) * VT_ROWS, kt:kt + KT],
                            p.astype(BF16), preferred_element_type=F32)
                pv = d if pv is None else pv + d
            l_scr[hd:hd + 1, :] = alpha * l_scr[hd:hd + 1, :] + pv[HEAD_DIM:HEAD_DIM + 1, :]
            m_scr[hd:hd + 1, :] = m_new
            rows = slice(hd * HEAD_DIM, (hd + 1) * HEAD_DIM)
            acc_scr[rows, :] = alpha * acc_scr[rows, :] + pv[0:HEAD_DIM, :]

    m_scr[...] = jnp.full((ATTN_HEADS, TQ), NEG_INF, F32)
    l_scr[...] = jnp.zeros((ATTN_HEADS, TQ), F32)
    acc_scr[...] = jnp.zeros((ATTN_WIDTH, TQ), F32)
    for j in range(N_PAIRS):
        build_qw(j)
        logits_pair(0, j, lga_scr)
    last = n_chunks - 1

    def attn_half(c, src, dst):
        c_next = jnp.minimum(c + 1, last)
        for j in range(N_PAIRS):
            logits_pair(c_next, j, dst)
            softmax_heads(c, src, (2 * j,))
            score_pair(c, j, witn_ref, nxt_t0)
            softmax_heads(c, src, (2 * j + 1,))

    def attn_body(t, carry):
        c = 2 * t
        attn_half(c, lga_scr, lgb_scr)

        @pl.when(c < last)
        def _():
            attn_half(c + 1, lgb_scr, lga_scr)
        return carry

    lax.fori_loop(0, lax.shift_right_logical(n_chunks + 1, 1), attn_body, 0)

    @pl.when(nxt_chunks > n_chunks)
    def _():
        score_chunk(n_chunks, witn_ref, nxt_t0)

    normed = []
    for hd in range(ATTN_HEADS):
        rows = slice(hd * HEAD_DIM, (hd + 1) * HEAD_DIM)
        o = acc_scr[rows, :] / l_scr[hd:hd + 1, :]
        ms = jnp.mean(o * o, axis=0, keepdims=True)
        normed.append(o * lax.rsqrt(ms + NORM_EPS))
    attn_t = jnp.concatenate(normed, axis=0)
    ya = attn_t.T * gattn_ref[...] * gz_ref[...]
    upd = (jnp.dot(yc_ref[...], wout_ref[0:CONV_WIDTH, :], preferred_element_type=F32)
           + jnp.dot(ya.astype(BF16), wout_ref[CONV_WIDTH:, :],
                     preferred_element_type=F32))
    xr = x_ref[...] + gate_ref[...] * upd
    ms = jnp.mean(xr * xr, axis=-1, keepdims=True)
    o_ref[...] = xr * lax.rsqrt(ms + NORM_EPS) * gfin_ref[...]


def _attention(x, yc, gz, q, qi, wit, k, vt, ki2, gate, wout, gattn, gfin):
    B, S, D = x.shape
    topk = min(TOPK_MAX, S // 4)
    n_q = S // TQ
    tok = lambda w: pl.BlockSpec((None, TQ, w), lambda b, i: (b, i, 0))
    const2 = lambda b, i: (0, 0)
    once = dict(pipeline_mode=pl.Buffered(1))
    nxt_tok = lambda b, i: (b, jnp.minimum(i + 1, n_q - 1), 0)
    nxt_row = lambda b, i: (b, 0, jnp.minimum(i + 1, n_q - 1))
    return pl.pallas_call(
        functools.partial(_attn_kernel, topk, S),
        grid=(B, n_q),
        in_specs=[
            tok(D),
            tok(CONV_WIDTH),
            tok(ATTN_WIDTH),
            tok(ATTN_WIDTH),
            tok(IDX_HEADS * IDX_DIM),
            pl.BlockSpec((None, IDX_HEADS, TQ), lambda b, i: (b, 0, i)),
            pl.BlockSpec((None, TQ, IDX_HEADS * IDX_DIM), nxt_tok),
            pl.BlockSpec((None, IDX_HEADS, TQ), nxt_row),
            pl.BlockSpec((None, S, ATTN_WIDTH), lambda b, i: (b, 0, 0)),
            pl.BlockSpec((None, S // CK, ATTN_HEADS * VT_ROWS, CK), lambda b, i: (b, 0, 0, 0)),
            pl.BlockSpec((None, S, 2 * IDX_DIM), lambda b, i: (b, 0, 0)),
            pl.BlockSpec((None, 1, D), lambda b, i: (b, 0, 0)),
            pl.BlockSpec((D, D), const2, **once),
            pl.BlockSpec((1, ATTN_WIDTH), const2),
            pl.BlockSpec((1, D), const2),
        ],
        out_specs=tok(D),
        out_shape=jax.ShapeDtypeStruct((B, S, D), F32),
        scratch_shapes=[
            pltpu.VMEM((S, TQ), F32),
            pltpu.VMEM((S, TQ), BF16),
            pltpu.VMEM((CK, TQ), F32),
            pltpu.VMEM((S, TQ), I16),
            pltpu.VMEM((S, TQ), F32),
            pltpu.VMEM((ATTN_HEADS, CK, TQ), F32),
            pltpu.VMEM((ATTN_HEADS, CK, TQ), F32),
            pltpu.VMEM((ATTN_WIDTH, TQ), F32),
            pltpu.VMEM((ATTN_HEADS, TQ), F32),
            pltpu.VMEM((ATTN_HEADS, TQ), F32),
            pltpu.VMEM((S, LANES), BF16),
            pltpu.VMEM((N_PAIRS, 2 * TQ, LANES), BF16),
            pltpu.VMEM((N_PAIRS, 2 * TQ, 2 * LANES), BF16),
        ],
        compiler_params=pltpu.CompilerParams(
            dimension_semantics=("arbitrary", "arbitrary"),
            vmem_limit_bytes=VMEM_LIMIT_BYTES),
        name="dsa_attn_out",
    )(x, yc, gz, q, qi, wit, qi, wit, k, vt, ki2, gate, wout, gattn, gfin)


def kernel(x, c, w_ada, b_ada, w_in, conv_w, conv_b, idx_k_gain, idx_k_bias,
           mix_norm_gain, w_out, final_gain):
    B, S, D = x.shape
    assert w_ada.shape[0] == 1 and B <= SUBLANES
    assert D == D_MODEL and S % TM == 0 and TM == CK and CK % TQ == 0 and CK == 2 ** LOG2_CK
    assert S <= NO_TIE
    c_pad = jnp.zeros((SUBLANES, D), F32).at[:B].set(c)
    grp = jnp.arange(CONV_WIDTH) // GROUP_DIM
    gmat = jnp.where(grp[:, None] == grp[None, :], 1.0 / GROUP_DIM, 0.0).astype(BF16)
    col = lambda w, a, n: w[:, a:a + n]
    o_q = 4 * CONV_WIDTH
    o_k, o_v, o_az = o_q + ATTN_WIDTH, o_q + 2 * ATTN_WIDTH, o_q + 3 * ATTN_WIDTH
    o_qi = o_q + 4 * ATTN_WIDTH
    o_ki = o_qi + IDX_HEADS * IDX_DIM
    o_wi = o_ki + IDX_DIM
    wl = w_in[0]
    w1 = jnp.concatenate(
        [col(wl, 0, 4 * CONV_WIDTH), col(wl, o_q, ATTN_WIDTH), col(wl, o_k, ATTN_WIDTH),
         col(wl, o_az, ATTN_WIDTH), col(wl, o_qi, IDX_HEADS * IDX_DIM),
         col(wl, o_ki, IDX_DIM), col(wl, o_ki, IDX_DIM)], axis=1).astype(BF16)
    w2t = jnp.concatenate(
        [col(wl, o_v, ATTN_WIDTH).T, col(wl, o_wi, IDX_HEADS).T,
         jnp.zeros((_W2_ROWS - ATTN_WIDTH - IDX_HEADS, D), F32)], axis=0).astype(BF16)
    mod = _modulation(c_pad, w_ada[0], b_ada[0][None, :])[:B]
    mod3 = mod.reshape(B, 3, D)
    dup = lambda v: jnp.concatenate([v, v])[None, :]
    yc, q, k, gz, qi, ki2, vt, wit = _projection(
        x, mod3, w1, w2t, conv_w[0], conv_b[0][None, :],
        dup(idx_k_gain[0]), dup(idx_k_bias[0]),
        mix_norm_gain[0][None, :CONV_WIDTH], gmat)
    return _attention(x, yc, gz, q, qi, wit, k, vt, ki2, mod3[:, 2:3, :],
                      w_out[0].astype(BF16), mix_norm_gain[0][None, CONV_WIDTH:],
                      final_gain[None, :])
```

```python
import functools

import numpy as np
import jax
import jax.numpy as jnp
from jax import lax
from jax.experimental import pallas as pl
from jax.experimental.pallas import tpu as pltpu

F32 = jnp.float32
BF16 = jnp.bfloat16
I32 = jnp.int32
I16 = jnp.int16

D_MODEL = 1024
CONV_WIDTH = 512
CONV_K = 3
ATTN_HEADS = 8
HEAD_DIM = 64
ATTN_WIDTH = ATTN_HEADS * HEAD_DIM
IDX_HEADS = 8
IDX_DIM = 64
TOPK_MAX = 256
GROUP_DIM = 64
NORM_EPS = 1e-6
NEG_INF = -1e30

LANES = 128
SUBLANES = 8
PACK16 = 2 * SUBLANES
VT_ROWS = HEAD_DIM + PACK16
VMEM_LIMIT_BYTES = 56 * 1024 * 1024

TM = 512
HM = TM // 2
TQ = 256
CK = 512
LOG2_CK = 9
KT = 256
N_PAIRS = ATTN_HEADS // 2

I16_MIN = -(2 ** 15)
NO_TIE = 2 ** 15 - 1
NEG_INF_BF16 = float(np.asarray(NEG_INF, dtype=BF16).astype(np.float32))

NT_DIMS = (((1,), (1,)), ((), ()))


def _alibi_slopes(n_heads):
    return [2.0 ** (-8.0 * (i + 1) / n_heads) for i in range(n_heads)]


def _mod_kernel(c_ref, w_ref, b_ref, o_ref):
    c_act = jax.nn.silu(c_ref[...])
    o_ref[...] = jnp.dot(c_act.astype(BF16), w_ref[...].astype(BF16),
                         preferred_element_type=F32) + b_ref[...]


def _modulation(c_pad, w_ada, b_ada):
    rows, d = c_pad.shape
    n_out = w_ada.shape[1]
    return pl.pallas_call(
        _mod_kernel,
        grid=(n_out // d,),
        in_specs=[
            pl.BlockSpec((rows, d), lambda j: (0, 0)),
            pl.BlockSpec((d, d), lambda j: (0, j)),
            pl.BlockSpec((1, d), lambda j: (0, j)),
        ],
        out_specs=pl.BlockSpec((rows, d), lambda j: (0, j)),
        out_shape=jax.ShapeDtypeStruct((rows, n_out), F32),
        compiler_params=pltpu.CompilerParams(
            dimension_semantics=("arbitrary",), vmem_limit_bytes=VMEM_LIMIT_BYTES),
        name="adaln_mod",
    )(c_pad, w_ada, b_ada)


_C_CONV = 0
_C_Q = 4 * CONV_WIDTH
_C_K = _C_Q + ATTN_WIDTH
_C_AZ = _C_K + ATTN_WIDTH
_C_QI = _C_AZ + ATTN_WIDTH
_C_KI = _C_QI + IDX_HEADS * IDX_DIM
_W1_COLS = _C_KI + 2 * IDX_DIM
_W2_ROWS = ATTN_WIDTH + 16


def _proj_kernel(x_ref, mod_ref, w1_ref, w2t_ref, cw_ref, cb_ref, lng_ref, lnb_ref,
                 gconv_ref, gmat_ref,
                 yc_ref, q_ref, k_ref, gz_ref, qi_ref, ki2_ref, vt_ref, wit_ref,
                 u_scr):
    i = pl.program_id(1)
    shift = mod_ref[0:1, :]
    scale = mod_ref[1:2, :]
    idx_scale = (IDX_DIM ** -0.5) * (IDX_HEADS ** -0.5)
    ones_rows = jnp.where(lax.broadcasted_iota(I32, (PACK16, HM), 0) == 0, 1.0, 0.0).astype(BF16)

    halves = []
    for r0 in (0, HM):
        x = x_ref[r0:r0 + HM, :]
        ms = jnp.mean(x * x, axis=-1, keepdims=True)
        h = (x * lax.rsqrt(ms + NORM_EPS) * (1.0 + scale) + shift).astype(BF16)
        pc = jnp.dot(h, w1_ref[:, _C_CONV:_C_CONV + 4 * CONV_WIDTH],
                     preferred_element_type=F32)
        halves.append((r0, h, pc))

    @pl.when(i == 0)
    def _():
        u_scr[0:SUBLANES, :] = jnp.zeros((SUBLANES, CONV_WIDTH), F32)

    for r0, h, pc in halves:
        rows = slice(r0, r0 + HM)

        def proj(c0, width):
            return jnp.dot(h, w1_ref[:, c0:c0 + width], preferred_element_type=F32)

        b_gate = pc[:, 0:CONV_WIDTH]
        c_gate = pc[:, CONV_WIDTH:2 * CONV_WIDTH]
        x_in = pc[:, 2 * CONV_WIDTH:3 * CONV_WIDTH]
        z_conv = pc[:, 3 * CONV_WIDTH:4 * CONV_WIDTH]
        u = c_gate * x_in
        u_scr[SUBLANES + r0:SUBLANES + r0 + HM, :] = u
        u_m1 = u_scr[SUBLANES - 1 + r0:SUBLANES - 1 + r0 + HM, :]
        u_m2 = u_scr[SUBLANES - 2 + r0:SUBLANES - 2 + r0 + HM, :]
        y = cb_ref[...] + cw_ref[0:1, :] * u_m2
        y = y + cw_ref[1:2, :] * u_m1
        y = y + cw_ref[2:3, :] * u
        conv_out = b_gate * y
        sq = conv_out * conv_out
        sq_hi = sq.astype(BF16)
        sq_lo = (sq - sq_hi.astype(F32)).astype(BF16)
        gms = (jnp.dot(sq_hi, gmat_ref[...], preferred_element_type=F32)
               + jnp.dot(sq_lo, gmat_ref[...], preferred_element_type=F32))
        yc = conv_out * lax.rsqrt(gms + NORM_EPS) * gconv_ref[...] * jax.nn.silu(z_conv)
        yc_ref[rows, :] = yc.astype(BF16)

        q_ref[rows, :] = (proj(_C_Q, ATTN_WIDTH) * (HEAD_DIM ** -0.5)).astype(BF16)
        k_ref[rows, :] = proj(_C_K, ATTN_WIDTH).astype(BF16)
        gz_ref[rows, :] = jax.nn.silu(proj(_C_AZ, ATTN_WIDTH))
        qi_ref[rows, :] = proj(_C_QI, IDX_HEADS * IDX_DIM).astype(BF16)

        pki = proj(_C_KI, 2 * IDX_DIM)
        mu = jnp.mean(pki, axis=-1, keepdims=True)
        var = jnp.mean(jnp.square(pki - mu), axis=-1, keepdims=True)
        ki = (pki - mu) * lax.rsqrt(var + NORM_EPS) * lng_ref[...] + lnb_ref[...]
        ki2_ref[rows, :] = ki.astype(BF16)

        pt = lax.dot_general(w2t_ref[...], h, NT_DIMS, preferred_element_type=F32)
        for hd in range(ATTN_HEADS):
            vt_ref[hd * VT_ROWS:hd * VT_ROWS + HEAD_DIM, rows] = pt[
                hd * HEAD_DIM:(hd + 1) * HEAD_DIM, :].astype(BF16)
            vt_ref[hd * VT_ROWS + HEAD_DIM:(hd + 1) * VT_ROWS, rows] = ones_rows
        wit_ref[:, rows] = pt[ATTN_WIDTH:ATTN_WIDTH + IDX_HEADS, :] * idx_scale

    u_scr[0:SUBLANES, :] = u_scr[TM:TM + SUBLANES, :]


def _projection(x, mod3, w1, w2t, conv_w, conv_b, lng2, lnb2, gconv, gmat):
    B, S, D = x.shape
    n_t = S // TM
    const2 = lambda b, i: (0, 0)
    out_shapes = (
        jax.ShapeDtypeStruct((B, S, CONV_WIDTH), BF16),
        jax.ShapeDtypeStruct((B, S, ATTN_WIDTH), BF16),
        jax.ShapeDtypeStruct((B, S, ATTN_WIDTH), BF16),
        jax.ShapeDtypeStruct((B, S, ATTN_WIDTH), F32),
        jax.ShapeDtypeStruct((B, S, IDX_HEADS * IDX_DIM), BF16),
        jax.ShapeDtypeStruct((B, S, 2 * IDX_DIM), BF16),
        jax.ShapeDtypeStruct((B, n_t, ATTN_HEADS * VT_ROWS, TM), BF16),
        jax.ShapeDtypeStruct((B, IDX_HEADS, S), F32),
    )
    tok = lambda w: pl.BlockSpec((None, TM, w), lambda b, i: (b, i, 0))
    return pl.pallas_call(
        _proj_kernel,
        grid=(B, n_t),
        in_specs=[
            pl.BlockSpec((None, TM, D), lambda b, i: (b, i, 0)),
            pl.BlockSpec((None, 3, D), lambda b, i: (b, 0, 0)),
            pl.BlockSpec((D, _W1_COLS), const2),
            pl.BlockSpec((_W2_ROWS, D), const2),
            pl.BlockSpec((CONV_K, CONV_WIDTH), const2),
            pl.BlockSpec((1, CONV_WIDTH), const2),
            pl.BlockSpec((1, 2 * IDX_DIM), const2),
            pl.BlockSpec((1, 2 * IDX_DIM), const2),
            pl.BlockSpec((1, CONV_WIDTH), const2),
            pl.BlockSpec((CONV_WIDTH, CONV_WIDTH), const2),
        ],
        out_specs=(
            tok(CONV_WIDTH), tok(ATTN_WIDTH), tok(ATTN_WIDTH), tok(ATTN_WIDTH),
            tok(IDX_HEADS * IDX_DIM), tok(2 * IDX_DIM),
            pl.BlockSpec((None, None, ATTN_HEADS * VT_ROWS, TM), lambda b, i: (b, i, 0, 0)),
            pl.BlockSpec((None, IDX_HEADS, TM), lambda b, i: (b, 0, i)),
        ),
        out_shape=out_shapes,
        scratch_shapes=[pltpu.VMEM((TM + 2 * SUBLANES, CONV_WIDTH), F32)],
        compiler_params=pltpu.CompilerParams(
            dimension_semantics=("arbitrary", "arbitrary"),
            vmem_limit_bytes=VMEM_LIMIT_BYTES),
        name="proj_conv",
    )(x, mod3, w1, w2t, conv_w, conv_b, lng2, lnb2, gconv, gmat)


def _tree(op, xs):
    xs = list(xs)
    while len(xs) > 1:
        xs = [op(xs[i], xs[i + 1]) if i + 1 < len(xs) else xs[i]
              for i in range(0, len(xs), 2)]
    return xs[0]


def _fold_rows(op, x, rows):
    return _tree(op, [x[r:r + rows] for r in range(0, x.shape[0], rows)])


def _attn_kernel(topk, seq_len,
                 x_ref, yc_ref, gz_ref, q_ref, qi_ref, wit_ref, qin_ref, witn_ref,
                 k_ref, vt_ref, ki2_ref, gate_ref, wout_ref, gattn_ref, gfin_ref,
                 o_ref,
                 sc_scr, sb_scr, accn_scr, tie_scr, mb_scr, lga_scr, lgb_scr, acc_scr, m_scr, l_scr, pos_scr,
                 qiw_scr, qw_scr):
    qb = pl.program_id(1)
    t0 = qb * TQ
    n_chunks = lax.shift_right_logical(t0 + (TQ - 1), LOG2_CK) + 1
    slopes = _alibi_slopes(ATTN_HEADS)

    def chunk_loop(body, init):
        return lax.fori_loop(
            0, n_chunks, lambda c, carry: body(c, pl.multiple_of(c * CK, CK), carry), init)

    lane_k = lax.broadcasted_iota(I32, (CK, LANES), 1)
    row_k = lax.broadcasted_iota(I32, (CK, LANES), 0)

    @pl.when(qb == 0)
    def _():
        def body(c, carry):
            base = pl.multiple_of(c * CK, CK)
            s = base + row_k
            val = jnp.where(lane_k == 0, lax.shift_right_logical(s, 6),
                            jnp.where(lane_k == 1, s & 63,
                                      jnp.where(lane_k <= 3, 1, 0)))
            pos_scr[pl.ds(base, CK), :] = val.astype(F32).astype(BF16)
            return carry
        lax.fori_loop(0, seq_len // CK, body, 0)

    lane = lax.broadcasted_iota(I32, (TQ, LANES), 1)
    t_row = t0 + lax.broadcasted_iota(I32, (TQ, LANES), 0)
    t_hi = lax.shift_right_logical(t_row, 6).astype(F32)
    t_lo = (t_row & 63).astype(F32)
    pcol = jnp.where(lane == 0, 64.0,
                     jnp.where(lane == 1, 1.0,
                               jnp.where(lane == 2, -64.0 * t_hi,
                                         jnp.where(lane == 3, -t_lo, 0.0))))
    lo_half = lane < HEAD_DIM

    def build_qw(j):
        pair = q_ref[:, j * LANES:(j + 1) * LANES]
        zero = jnp.zeros_like(pair)
        for e in range(2):
            rows = slice(e * TQ, (e + 1) * TQ)
            keep = lo_half if e == 0 else jnp.logical_not(lo_half)
            qw_scr[j, rows, 0:LANES] = jnp.where(keep, pair, zero)
            qw_scr[j, rows, LANES:2 * LANES] = (slopes[2 * j + e] * pcol).astype(BF16)

    row = lax.broadcasted_iota(I32, (CK, TQ), 0)
    q_lane = lax.broadcasted_iota(I32, (CK, TQ), 1)
    t_idx = t0 + q_lane

    def build_qiw(qi_src):
        for j in range(N_PAIRS):
            pair = qi_src[:, j * LANES:(j + 1) * LANES]
            zero = jnp.zeros_like(pair)
            qiw_scr[j, 0:TQ, :] = jnp.where(lo_half, pair, zero)
            qiw_scr[j, TQ:2 * TQ, :] = jnp.where(lo_half, zero, pair)

    def score_pair(c, j, wit_src, blk_t0):
        base = pl.multiple_of(c * CK, CK)
        d = lax.dot_general(ki2_ref[pl.ds(base, CK), :], qiw_scr[j], NT_DIMS,
                            preferred_element_type=F32)
        r = jnp.maximum(d, 0.0)
        term = (r[:, 0:TQ] * wit_src[2 * j:2 * j + 1, :]
                + r[:, TQ:2 * TQ] * wit_src[2 * j + 1:2 * j + 2, :])
        if j == 0:
            accn_scr[...] = term
        elif j < N_PAIRS - 1:
            accn_scr[...] += term
        else:
            score = jnp.where(base + row <= blk_t0 + q_lane, accn_scr[...] + term, -jnp.inf)
            sc_scr[pl.ds(base, CK), :] = score
            sb_scr[pl.ds(base, CK), :] = score.astype(BF16)

    def score_chunk(c, wit_src, blk_t0):
        for j in range(N_PAIRS):
            score_pair(c, j, wit_src, blk_t0)

    @pl.when(qb == 0)
    def _():
        build_qiw(qi_ref)

        def body(c, carry):
            score_chunk(c, wit_ref, t0)
            return carry
        lax.fori_loop(0, n_chunks, body, 0)

    nxt_t0 = jnp.minimum(qb + 1, pl.num_programs(1) - 1) * TQ
    nxt_chunks = lax.shift_right_logical(nxt_t0 + (TQ - 1), LOG2_CK) + 1
    build_qiw(qin_ref)

    n_acc = 4
    one16 = jnp.ones((PACK16, TQ), I16)
    zero16 = jnp.zeros((PACK16, TQ), I16)
    zero_row = jnp.zeros((1, TQ), I32)
    n_noncausal = (seq_len - 1) - (t0 + lax.broadcasted_iota(I32, (1, TQ), 1))

    def key_to_f32(key):
        return pltpu.bitcast(jnp.where(key >= 0, key, key ^ 0x7FFFFFFF), F32)

    def count_bf16(cand):
        cand16 = jnp.broadcast_to(cand, (PACK16, TQ)).astype(BF16)

        def body(c, base, accs):
            accs = list(accs)
            blk = sb_scr[pl.ds(base, CK), :]
            for r in range(CK // PACK16):
                hit = blk[r * PACK16:(r + 1) * PACK16] >= cand16
                accs[r % n_acc] = accs[r % n_acc] + jnp.where(hit, one16, zero16)
            return tuple(accs)
        accs = chunk_loop(body, (zero16,) * n_acc)
        tot = ((accs[0] + accs[1]) + (accs[2] + accs[3])).astype(I32)
        return jnp.sum(tot, axis=0, keepdims=True)

    def count_f32(hit_fn):
        zero8 = jnp.zeros((SUBLANES, TQ), I32)

        def body(c, base, accs):
            accs = list(accs)
            blk = sc_scr[pl.ds(base, CK), :]
            for r in range(CK // SUBLANES):
                hit = hit_fn(blk[r * SUBLANES:(r + 1) * SUBLANES], base + r * SUBLANES)
                accs[r % n_acc] = accs[r % n_acc] + jnp.where(hit, 1, 0)
            return tuple(accs)
        accs = chunk_loop(body, (zero8,) * n_acc)
        return jnp.sum((accs[0] + accs[1]) + (accs[2] + accs[3]), axis=0, keepdims=True)

    def bf16_key(u16):
        k16 = u16 + I16_MIN
        return lax.shift_left(k16, 16) + jnp.where(k16 < 0, 0xFFFF, 0)

    def coarse_body(ib, u_thr):
        cand_u = u_thr | lax.shift_left(jnp.int32(1), 15 - ib)
        cand = key_to_f32(bf16_key(cand_u))
        total = count_bf16(cand) + jnp.where(cand <= NEG_INF_BF16, n_noncausal, 0)
        return jnp.where(total >= topk, cand_u, u_thr)

    def search():
        u_coarse = lax.fori_loop(0, 16, coarse_body, zero_row)
        base_key = bf16_key(u_coarse) - 65536

        def fine_body(ib, off):
            cand_off = off | lax.shift_left(jnp.int32(1), 16 - ib)
            cand = key_to_f32(base_key + cand_off)
            total = (count_f32(lambda sv, _: sv >= cand)
                     + jnp.where(cand <= NEG_INF, n_noncausal, 0))
            return jnp.where(total >= topk, cand_off, off)

        return key_to_f32(base_key + lax.fori_loop(0, 17, fine_body, zero_row))

    thr = lax.cond(t0 + TQ <= topk,
                   lambda: jnp.full((1, TQ), NEG_INF, F32), search)

    def mask_body(c, base, acc):
        hit = sc_scr[pl.ds(base, CK), :] >= thr
        mb_scr[pl.ds(base, CK), :] = jnp.where(hit, 0.0, NEG_INF)
        return acc + _fold_rows(jnp.add, jnp.where(hit, 1, 0), SUBLANES)

    cnt_thr = jnp.sum(chunk_loop(mask_body, jnp.zeros((SUBLANES, TQ), I32)),
                      axis=0, keepdims=True)

    def mask_ties():
        def tie_body(c, base, carry):
            tied = sc_scr[pl.ds(base, CK), :] == thr
            tie_scr[pl.ds(base, CK), :] = jnp.where(tied, base + row, NO_TIE).astype(I16)
            return carry
        chunk_loop(tie_body, 0)

        def count_ties_below(idx):
            idx16 = jnp.broadcast_to(idx, (PACK16, TQ)).astype(I16)

            def body(c, base, accs):
                accs = list(accs)
                blk = tie_scr[pl.ds(base, CK), :]
                for r in range(CK // PACK16):
                    hit = blk[r * PACK16:(r + 1) * PACK16] < idx16
                    accs[r % n_acc] = accs[r % n_acc] + jnp.where(hit, one16, zero16)
                return tuple(accs)
            accs = chunk_loop(body, (zero16,) * n_acc)
            tot = ((accs[0] + accs[1]) + (accs[2] + accs[3])).astype(I32)
            return jnp.sum(tot, axis=0, keepdims=True)

        n_tied = count_ties_below(jnp.full((1, TQ), NO_TIE, I32))
        need = topk - (cnt_thr - n_tied)
        n_bits = LOG2_CK + sum((n_chunks > 2 ** e).astype(I32)
                               for e in range((seq_len // CK - 1).bit_length()))

        def idx_body(ib, j_thr):
            cand = j_thr | lax.shift_left(jnp.int32(1), (n_bits - 1) - ib)
            return jnp.where(count_ties_below(cand) < need, cand, j_thr)

        j_thr = lax.fori_loop(0, n_bits, idx_body, zero_row)

        def body(c, base, carry):
            kv = sc_scr[pl.ds(base, CK), :]
            tie_ok = jnp.where(base + row <= j_thr, 0.0, NEG_INF)
            mb_scr[pl.ds(base, CK), :] = jnp.where(
                kv > thr, 0.0, jnp.where(kv == thr, tie_ok, NEG_INF))
            return carry
        chunk_loop(body, 0)

    lax.cond(jnp.max(cnt_thr) > topk, mask_ties, lambda: None)

    def logits_pair(c, j, dst):
        base = pl.multiple_of(c * CK, CK)
        mb = mb_scr[pl.ds(base, CK), :]
        lhs = jnp.concatenate([k_ref[pl.ds(base, CK), j * LANES:(j + 1) * LANES],
                               pos_scr[pl.ds(base, CK), :]], axis=1)
        lg = lax.dot_general(lhs, qw_scr[j], NT_DIMS, preferred_element_type=F32)
        for e in range(2):
            dst[2 * j + e] = lg[:, e * TQ:(e + 1) * TQ] + mb

    def logits_chunk(c, dst):
        for j in range(N_PAIRS):
            logits_pair(c, j, dst)

    def softmax_heads(c, src, heads):
        for hd in heads:
            m_old = m_scr[hd:hd + 1, :]
            slab_max = _tree(jnp.maximum, [
                _fold_rows(jnp.maximum, src[hd, r:r + KT // 4, :], SUBLANES)
                for r in range(0, CK, KT // 4)])
            m_new = jnp.maximum(m_old, jnp.max(slab_max, axis=0, keepdims=True))
            alpha = jnp.exp(m_old - m_new)
            pv = None
            for kt in range(0, CK, KT):
                p = jnp.exp(src[hd, kt:kt + KT, :] - m_new)
                d = jnp.dot(vt_ref[c, hd * VT_ROWS:(hd + 1) * VT_ROWS, kt:kt + KT],
                            p.astype(BF16), preferred_element_type=F32)
                pv = d if pv is None else pv + d
            l_scr[hd:hd + 1, :] = alpha * l_scr[hd:hd + 1, :] + pv[HEAD_DIM:HEAD_DIM + 1, :]
            m_scr[hd:hd + 1, :] = m_new
            rows = slice(hd * HEAD_DIM, (hd + 1) * HEAD_DIM)
            acc_scr[rows, :] = alpha * acc_scr[rows, :] + pv[0:HEAD_DIM, :]

    m_scr[...] = jnp.full((ATTN_HEADS, TQ), NEG_INF, F32)
    l_scr[...] = jnp.zeros((ATTN_HEADS, TQ), F32)
    acc_scr[...] = jnp.zeros((ATTN_WIDTH, TQ), F32)
    for j in range(N_PAIRS):
        build_qw(j)
        logits_pair(0, j, lga_scr)
    last = n_chunks - 1

    def attn_half(c, src, dst):
        c_next = jnp.minimum(c + 1, last)
        for j in range(N_PAIRS):
            logits_pair(c_next, j, dst)
            softmax_heads(c, src, (2 * j,))
            score_pair(c, j, witn_ref, nxt_t0)
            softmax_heads(c, src, (2 * j + 1,))

    def attn_body(t, carry):
        c = 2 * t
        attn_half(c, lga_scr, lgb_scr)

        @pl.when(c < last)
        def _():
            attn_half(c + 1, lgb_scr, lga_scr)
        return carry

    lax.fori_loop(0, lax.shift_right_logical(n_chunks + 1, 1), attn_body, 0)

    @pl.when(nxt_chunks > n_chunks)
    def _():
        score_chunk(n_chunks, witn_ref, nxt_t0)

    normed = []
    for hd in range(ATTN_HEADS):
        rows = slice(hd * HEAD_DIM, (hd + 1) * HEAD_DIM)
        o = acc_scr[rows, :] / l_scr[hd:hd + 1, :]
        ms = jnp.mean(o * o, axis=0, keepdims=True)
        normed.append(o * lax.rsqrt(ms + NORM_EPS))
    attn_t = jnp.concatenate(normed, axis=0)
    ya = attn_t.T * gattn_ref[...] * gz_ref[...]
    upd = (jnp.dot(yc_ref[...], wout_ref[0:CONV_WIDTH, :], preferred_element_type=F32)
           + jnp.dot(ya.astype(BF16), wout_ref[CONV_WIDTH:, :],
                     preferred_element_type=F32))
    xr = x_ref[...] + gate_ref[...] * upd
    ms = jnp.mean(xr * xr, axis=-1, keepdims=True)
    o_ref[...] = xr * lax.rsqrt(ms + NORM_EPS) * gfin_ref[...]


def _attention(x, yc, gz, q, qi, wit, k, vt, ki2, gate, wout, gattn, gfin):
    B, S, D = x.shape
    topk = min(TOPK_MAX, S // 4)
    n_q = S // TQ
    tok = lambda w: pl.BlockSpec((None, TQ, w), lambda b, i: (b, i, 0))
    const2 = lambda b, i: (0, 0)
    once = dict(pipeline_mode=pl.Buffered(1))
    nxt_tok = lambda b, i: (b, jnp.minimum(i + 1, n_q - 1), 0)
    nxt_row = lambda b, i: (b, 0, jnp.minimum(i + 1, n_q - 1))
    return pl.pallas_call(
        functools.partial(_attn_kernel, topk, S),
        grid=(B, n_q),
        in_specs=[
            tok(D),
            tok(CONV_WIDTH),
            tok(ATTN_WIDTH),
            tok(ATTN_WIDTH),
            tok(IDX_HEADS * IDX_DIM),
            pl.BlockSpec((None, IDX_HEADS, TQ), lambda b, i: (b, 0, i)),
            pl.BlockSpec((None, TQ, IDX_HEADS * IDX_DIM), nxt_tok),
            pl.BlockSpec((None, IDX_HEADS, TQ), nxt_row),
            pl.BlockSpec((None, S, ATTN_WIDTH), lambda b, i: (b, 0, 0)),
            pl.BlockSpec((None, S // CK, ATTN_HEADS * VT_ROWS, CK), lambda b, i: (b, 0, 0, 0)),
            pl.BlockSpec((None, S, 2 * IDX_DIM), lambda b, i: (b, 0, 0)),
            pl.BlockSpec((None, 1, D), lambda b, i: (b, 0, 0)),
            pl.BlockSpec((D, D), const2, **once),
            pl.BlockSpec((1, ATTN_WIDTH), const2),
            pl.BlockSpec((1, D), const2),
        ],
        out_specs=tok(D),
        out_shape=jax.ShapeDtypeStruct((B, S, D), F32),
        scratch_shapes=[
            pltpu.VMEM((S, TQ), F32),
            pltpu.VMEM((S, TQ), BF16),
            pltpu.VMEM((CK, TQ), F32),
            pltpu.VMEM((S, TQ), I16),
            pltpu.VMEM((S, TQ), F32),
            pltpu.VMEM((ATTN_HEADS, CK, TQ), F32),
            pltpu.VMEM((ATTN_HEADS, CK, TQ), F32),
            pltpu.VMEM((ATTN_WIDTH, TQ), F32),
            pltpu.VMEM((ATTN_HEADS, TQ), F32),
            pltpu.VMEM((ATTN_HEADS, TQ), F32),
            pltpu.VMEM((S, LANES), BF16),
            pltpu.VMEM((N_PAIRS, 2 * TQ, LANES), BF16),
            pltpu.VMEM((N_PAIRS, 2 * TQ, 2 * LANES), BF16),
        ],
        compiler_params=pltpu.CompilerParams(
            dimension_semantics=("arbitrary", "arbitrary"),
            vmem_limit_bytes=VMEM_LIMIT_BYTES),
        name="dsa_attn_out",
    )(x, yc, gz, q, qi, wit, qi, wit, k, vt, ki2, gate, wout, gattn, gfin)


def kernel(x, c, w_ada, b_ada, w_in, conv_w, conv_b, idx_k_gain, idx_k_bias,
           mix_norm_gain, w_out, final_gain):
    B, S, D = x.shape
    assert w_ada.shape[0] == 1 and B <= SUBLANES
    assert D == D_MODEL and S % TM == 0 and TM == CK and CK % TQ == 0 and CK == 2 ** LOG2_CK
    assert S <= NO_TIE
    c_pad = jnp.zeros((SUBLANES, D), F32).at[:B].set(c)
    grp = jnp.arange(CONV_WIDTH) // GROUP_DIM
    gmat = jnp.where(grp[:, None] == grp[None, :], 1.0 / GROUP_DIM, 0.0).astype(BF16)
    col = lambda w, a, n: w[:, a:a + n]
    o_q = 4 * CONV_WIDTH
    o_k, o_v, o_az = o_q + ATTN_WIDTH, o_q + 2 * ATTN_WIDTH, o_q + 3 * ATTN_WIDTH
    o_qi = o_q + 4 * ATTN_WIDTH
    o_ki = o_qi + IDX_HEADS * IDX_DIM
    o_wi = o_ki + IDX_DIM
    wl = w_in[0]
    w1 = jnp.concatenate(
        [col(wl, 0, 4 * CONV_WIDTH), col(wl, o_q, ATTN_WIDTH), col(wl, o_k, ATTN_WIDTH),
         col(wl, o_az, ATTN_WIDTH), col(wl, o_qi, IDX_HEADS * IDX_DIM),
         col(wl, o_ki, IDX_DIM), col(wl, o_ki, IDX_DIM)], axis=1).astype(BF16)
    w2t = jnp.concatenate(
        [col(wl, o_v, ATTN_WIDTH).T, col(wl, o_wi, IDX_HEADS).T,
         jnp.zeros((_W2_ROWS - ATTN_WIDTH - IDX_HEADS, D), F32)], axis=0).astype(BF16)
    mod = _modulation(c_pad, w_ada[0], b_ada[0][None, :])[:B]
    mod3 = mod.reshape(B, 3, D)
    dup = lambda v: jnp.concatenate([v, v])[None, :]
    yc, q, k, gz, qi, ki2, vt, wit = _projection(
        x, mod3, w1, w2t, conv_w[0], conv_b[0][None, :],
        dup(idx_k_gain[0]), dup(idx_k_bias[0]),
        mix_norm_gain[0][None, :CONV_WIDTH], gmat)
    return _attention(x, yc, gz, q, qi, wit, k, vt, ki2, mod3[:, 2:3, :],
                      w_out[0].astype(BF16), mix_norm_gain[0][None, CONV_WIDTH:],
                      final_gain[None, :])
```
